```python
import jax, jax.numpy as jnp
from jax import lax
import numpy as np

D_MODEL = 2048
BATCH = 4
SEQ = 2048
DEPTH = 4
DEC_BATCH = 32
DEC_SEQ = 1
PAST_LEN = 16384
PAGE_SIZE = 128

N_HEADS = 16
N_KV = 4
HEAD_DIM = 64
GQA = N_HEADS // N_KV
WINDOW = 128
ATTN_BLOCK = WINDOW
ROPE_THETA = 10000.0
CONV_DIM = D_MODEL // 2
CONV_WIDTH = 3
CHUNK = 128
CMLP_W = D_MODEL
CMLP_GROUPS = 16
CMLP_DG = CMLP_W // CMLP_GROUPS
FFN_W = 4 * D_MODEL
Q_W = N_HEADS * HEAD_DIM
KV_W = N_KV * HEAD_DIM
EVEN_PROJ = Q_W + 2 * KV_W + 3 * CONV_DIM
EVEN_CAT = Q_W + CONV_DIM
N_EVEN = (DEPTH + 1) // 2
N_ODD = DEPTH // 2
EPS = 1e-6

kernel_name = 'sink_swa_shortconv_chunkgmlp_step'


def rmsnorm(x, g):
    xf = x.astype(jnp.float32)
    y = xf * lax.rsqrt(jnp.mean(xf * xf, axis=-1, keepdims=True) + EPS)
    return (y * g.astype(jnp.float32)).astype(x.dtype)


def layernorm(x, g, b):
    xf = x.astype(jnp.float32)
    xc = xf - jnp.mean(xf, axis=-1, keepdims=True)
    var = jnp.mean(xc * xc, axis=-1, keepdims=True)
    return (xc * lax.rsqrt(var + EPS) * g.astype(jnp.float32) + b.astype(jnp.float32)).astype(x.dtype)


def rope(x, pos):
    half = HEAD_DIM // 2
    inv_freq = ROPE_THETA ** (-jnp.arange(half, dtype=jnp.float32) / half)
    ang = pos.astype(jnp.float32)[:, None] * inv_freq[None, :]
    cos = jnp.cos(ang)[:, None, :]
    sin = jnp.sin(ang)[:, None, :]
    xf = x.astype(jnp.float32)
    x1, x2 = xf[..., :half], xf[..., half:]
    return jnp.concatenate([x1 * cos - x2 * sin, x2 * cos + x1 * sin], axis=-1).astype(x.dtype)


def sink_softmax(scores, mask, sink):
    s = jnp.where(mask, scores, -jnp.inf)
    m = jnp.maximum(jnp.max(s, axis=-1, keepdims=True), sink)
    p = jnp.exp(s - m)
    return p / (jnp.sum(p, axis=-1, keepdims=True) + jnp.exp(sink - m))


def split_even(p):
    cuts = [Q_W, Q_W + KV_W, Q_W + 2 * KV_W, Q_W + 2 * KV_W + CONV_DIM, Q_W + 2 * KV_W + 2 * CONV_DIM]
    return jnp.split(p, cuts, axis=-1)


def causal_conv(zcat, w):
    t = zcat.shape[1] - (CONV_WIDTH - 1)
    out = w[0] * zcat[:, 0:t]
    for i in range(1, CONV_WIDTH):
        out = out + w[i] * zcat[:, i:i + t]
    return out


def window_attn_prompt(q, k, v, sinks):
    b, s = q.shape[:2]
    nb = s // ATTN_BLOCK
    qb = q.reshape(b, nb, ATTN_BLOCK, N_KV, GQA, HEAD_DIM)
    kb = k.reshape(b, nb, ATTN_BLOCK, N_KV, HEAD_DIM)
    vb = v.reshape(b, nb, ATTN_BLOCK, N_KV, HEAD_DIM)
    zk = jnp.zeros_like(kb[:, :1])
    kband = jnp.concatenate([jnp.concatenate([zk, kb[:, :-1]], axis=1), kb], axis=2)
    vband = jnp.concatenate([jnp.concatenate([zk, vb[:, :-1]], axis=1), vb], axis=2)
    scores = jnp.einsum('bnqkgd,bnskd->bnkgqs', qb, kband).astype(jnp.float32) * (HEAD_DIM ** -0.5)
    blk = jnp.arange(nb)[:, None] * ATTN_BLOCK
    qpos = blk + jnp.arange(ATTN_BLOCK)[None, :]
    kpos = blk - ATTN_BLOCK + jnp.arange(2 * ATTN_BLOCK)[None, :]
    diff = qpos[:, :, None] - kpos[:, None, :]
    mask = (diff >= 0) & (diff < WINDOW) & (kpos[:, None, :] >= 0)
    sink = sinks.astype(jnp.float32).reshape(N_KV, GQA)[None, None, :, :, None, None]
    probs = sink_softmax(scores, mask[None, :, None, None], sink)
    out = jnp.einsum('bnkgqs,bnskd->bnqkgd', probs.astype(v.dtype), vband)
    return out.reshape(b, s, Q_W)


def window_attn_sample(q, kall, vall, sinks):
    dbs, t = q.shape[:2]
    qg = q.reshape(dbs, t, N_KV, GQA, HEAD_DIM)
    scores = jnp.einsum('btkgd,bskd->bkgts', qg, kall).astype(jnp.float32) * (HEAD_DIM ** -0.5)
    qpos = PAST_LEN + jnp.arange(t)
    kpos = PAST_LEN - WINDOW + jnp.arange(WINDOW + t)
    diff = qpos[:, None] - kpos[None, :]
    mask = (diff >= 0) & (diff < WINDOW)
    sink = sinks.astype(jnp.float32).reshape(N_KV, GQA)[None, :, :, None, None]
    probs = sink_softmax(scores, mask[None, None, None], sink)
    out = jnp.einsum('bkgts,bskd->btkgd', probs.astype(vall.dtype), vall)
    return out.reshape(dbs, t, Q_W)


def even_prompt(h, w_in, w_out, conv_w, sinks):
    b, s, _ = h.shape
    q, k, v, gate_b, gate_c, h_conv = split_even(h @ w_in)
    pos = jnp.arange(s)
    q = rope(q.reshape(b, s, N_HEADS, HEAD_DIM), pos)
    k = rope(k.reshape(b, s, N_KV, HEAD_DIM), pos)
    v = v.reshape(b, s, N_KV, HEAD_DIM)
    attn = window_attn_prompt(q, k, v, sinks)
    z = gate_c * h_conv
    zcat = jnp.concatenate([jnp.zeros((b, CONV_WIDTH - 1, CONV_DIM), z.dtype), z], axis=1)
    conv = gate_b * causal_conv(zcat, conv_w)
    out = jnp.concatenate([attn, conv], axis=-1) @ w_out
    return out, k[:, s - WINDOW:], v[:, s - WINDOW:], z[:, s - (CONV_WIDTH - 1):]


def even_sample(h, cache_k, cache_v, conv_state, w_in, w_out, conv_w, sinks):
    dbs, t, _ = h.shape
    q, k, v, gate_b, gate_c, h_conv = split_even(h @ w_in)
    pos = PAST_LEN + jnp.arange(t)
    q = rope(q.reshape(dbs, t, N_HEADS, HEAD_DIM), pos)
    k = rope(k.reshape(dbs, t, N_KV, HEAD_DIM), pos)
    v = v.reshape(dbs, t, N_KV, HEAD_DIM)
    kall = jnp.concatenate([cache_k, k], axis=1)
    vall = jnp.concatenate([cache_v, v], axis=1)
    attn = window_attn_sample(q, kall, vall, sinks)
    z = gate_c * h_conv
    zcat = jnp.concatenate([conv_state, z], axis=1)
    conv = gate_b * causal_conv(zcat, conv_w)
    out = jnp.concatenate([attn, conv], axis=-1) @ w_out
    return out, kall[:, t:], vall[:, t:], zcat[:, t:]


def cmlp_uv(h, w_in, ln_g, ln_b):
    z = jax.nn.gelu(h @ w_in, approximate=False)
    u, v = jnp.split(z, 2, axis=-1)
    return u, layernorm(v, ln_g, ln_b)


def cmlp_prompt(h, w_in, w_out, ln_g, ln_b, w_s, b_s):
    b, s, _ = h.shape
    nc = s // CHUNK
    u, v = cmlp_uv(h, w_in, ln_g, ln_b)
    vg = v.reshape(b, nc, CHUNK, CMLP_GROUPS, CMLP_DG)
    ws = jnp.tril(w_s)
    mix = jnp.einsum('gts,bnsgd->bntgd', ws, vg) + b_s.T[None, None, :, :, None]
    gated = u.reshape(b, nc, CHUNK, CMLP_GROUPS, CMLP_DG) * mix
    return gated.reshape(b, s, CMLP_W) @ w_out, v[:, s - CHUNK:]


def cmlp_sample(h, w_in, w_out, ln_g, ln_b, w_s, b_s):
    dbs, t, _ = h.shape
    u, v = cmlp_uv(h, w_in, ln_g, ln_b)
    vg = v.reshape(dbs, t, CMLP_GROUPS, CMLP_DG)
    ws = jnp.tril(w_s[:, :t, :t])
    mix = jnp.einsum('gts,bsgd->btgd', ws, vg) + b_s[:, :t].T[None, :, :, None]
    gated = u.reshape(dbs, t, CMLP_GROUPS, CMLP_DG) * mix
    return gated.reshape(dbs, t, CMLP_W) @ w_out, v


def ffn(h, w_up, w_down):
    a = jax.nn.relu(h @ w_up)
    return (a * a) @ w_down


def setup_inputs(seed: int = 0) -> dict:
    key = jax.random.key(seed)
    ks = jax.random.split(key, 21)

    def nrm(k, shape, scale):
        return jax.random.normal(k, shape, jnp.float32) * scale

    return {
        'x_prompt': nrm(ks[0], (BATCH, SEQ, D_MODEL), 1.0),
        'x_sample': nrm(ks[1], (DEC_BATCH, DEC_SEQ, D_MODEL), 1.0),
        'cache_k_win': nrm(ks[2], (N_EVEN, DEC_BATCH, WINDOW, N_KV, HEAD_DIM), 1.0),
        'cache_v_win': nrm(ks[3], (N_EVEN, DEC_BATCH, WINDOW, N_KV, HEAD_DIM), 1.0),
        'state_conv': nrm(ks[4], (N_EVEN, DEC_BATCH, CONV_WIDTH - 1, CONV_DIM), 1.0),
        'w_in_even': nrm(ks[5], (N_EVEN, D_MODEL, EVEN_PROJ), D_MODEL ** -0.5),
        'w_out_even': nrm(ks[6], (N_EVEN, EVEN_CAT, D_MODEL), EVEN_CAT ** -0.5),
        'conv_w': nrm(ks[7], (N_EVEN, CONV_WIDTH, CONV_DIM), CONV_WIDTH ** -0.5),
        'attn_sinks': nrm(ks[8], (N_EVEN, N_HEADS), 0.5),
        'w_in_cmlp': nrm(ks[9], (N_ODD, D_MODEL, 2 * CMLP_W), D_MODEL ** -0.5),
        'w_out_cmlp': nrm(ks[10], (N_ODD, CMLP_W, D_MODEL), CMLP_W ** -0.5),
        'ln_v_g': 1.0 + nrm(ks[11], (N_ODD, CMLP_W), 0.1),
        'ln_v_b': nrm(ks[12], (N_ODD, CMLP_W), 0.02),
        'w_spatial': nrm(ks[13], (N_ODD, CMLP_GROUPS, CHUNK, CHUNK), CHUNK ** -0.5),
        'b_spatial': 1.0 + nrm(ks[14], (N_ODD, CMLP_GROUPS, CHUNK), 0.1),
        'w_ffn_up': nrm(ks[15], (DEPTH, D_MODEL, FFN_W), D_MODEL ** -0.5),
        'w_ffn_down': nrm(ks[16], (DEPTH, FFN_W, D_MODEL), FFN_W ** -0.5),
        'g_mix_pre': 1.0 + nrm(ks[17], (DEPTH, D_MODEL), 0.1),
        'g_mix_post': 1.0 + nrm(ks[18], (DEPTH, D_MODEL), 0.1),
        'g_ffn_pre': 1.0 + nrm(ks[19], (DEPTH, D_MODEL), 0.1),
        'g_ffn_post': 1.0 + nrm(ks[20], (DEPTH, D_MODEL), 0.1),
    }


def reference(x_prompt, x_sample, cache_k_win, cache_v_win, state_conv, w_in_even, w_out_even, conv_w,
              attn_sinks, w_in_cmlp, w_out_cmlp, ln_v_g, ln_v_b, w_spatial, b_spatial, w_ffn_up, w_ffn_down,
              g_mix_pre, g_mix_post, g_ffn_pre, g_ffn_post):
    xp, xs = x_prompt, x_sample
    kp_l, vp_l, cp_l, up_l = [], [], [], []
    ks_l, vs_l, cs_l, us_l = [], [], [], []
    for i in range(DEPTH):
        j = i // 2
        hp = rmsnorm(xp, g_mix_pre[i])
        hs = rmsnorm(xs, g_mix_pre[i])
        if i % 2 == 0:
            op, k_p, v_p, c_p = even_prompt(hp, w_in_even[j], w_out_even[j], conv_w[j], attn_sinks[j])
            osm, k_s, v_s, c_s = even_sample(hs, cache_k_win[j], cache_v_win[j], state_conv[j],
                                             w_in_even[j], w_out_even[j], conv_w[j], attn_sinks[j])
            kp_l.append(k_p); vp_l.append(v_p); cp_l.append(c_p)
            ks_l.append(k_s); vs_l.append(v_s); cs_l.append(c_s)
        else:
            op, r_p = cmlp_prompt(hp, w_in_cmlp[j], w_out_cmlp[j], ln_v_g[j], ln_v_b[j], w_spatial[j], b_spatial[j])
            osm, r_s = cmlp_sample(hs, w_in_cmlp[j], w_out_cmlp[j], ln_v_g[j], ln_v_b[j], w_spatial[j], b_spatial[j])
            up_l.append(r_p); us_l.append(r_s)
        xp = xp + rmsnorm(op, g_mix_post[i])
        xs = xs + rmsnorm(osm, g_mix_post[i])
        xp = xp + rmsnorm(ffn(rmsnorm(xp, g_ffn_pre[i]), w_ffn_up[i], w_ffn_down[i]), g_ffn_post[i])
        xs = xs + rmsnorm(ffn(rmsnorm(xs, g_ffn_pre[i]), w_ffn_up[i], w_ffn_down[i]), g_ffn_post[i])
    return (xp, xs, jnp.stack(kp_l), jnp.stack(vp_l), jnp.stack(cp_l), jnp.stack(up_l),
            jnp.stack(ks_l), jnp.stack(vs_l), jnp.stack(cs_l), jnp.stack(us_l))
```

```python
import functools
import math

import jax
import jax.numpy as jnp
from jax import lax
from jax.experimental import pallas as pl
from jax.experimental.pallas import tpu as pltpu

F32 = jnp.float32
BF16 = jnp.bfloat16

EPS = 1e-6
HEAD_DIM = 64
ROPE_THETA = 10000.0
PAST_LEN = 16384
CONV_WIDTH = 3

LANES = 128
SUBLANES = 8
VMEM_BYTES_V7X = 64 * 1024 * 1024

ROW_TILE = 1024


def _vmem_limit(block_bytes, temp_bytes):
    need = 2 * block_bytes + temp_bytes + (4 << 20)
    return int(min(need, VMEM_BYTES_V7X - (6 << 20)))


def _nbytes(shape, dtype):
    return math.prod(shape) * jnp.dtype(dtype).itemsize


def _rms_scale(x, g):
    ms = jnp.mean(x * x, axis=-1, keepdims=True)
    return x * lax.rsqrt(ms + EPS) * g


def _identity(y):
    return y


def _gelu_exact(y):
    return 0.5 * y * (1.0 + lax.erf(y * math.sqrt(0.5)))


def _relu_sq(y):
    r = jnp.maximum(y, 0.0)
    return r * r


def _norm_matmul_kernel(x_ref, g_ref, w_ref, o_ref, h_ref, *, act):
    @pl.when(pl.program_id(1) == 0)
    def _():
        h_ref[...] = _rms_scale(x_ref[...], g_ref[...]).astype(BF16)

    y = jnp.dot(h_ref[...], w_ref[...].astype(BF16), preferred_element_type=F32)
    o_ref[...] = act(y).astype(o_ref.dtype)


def _norm_matmul(x, g, g_layer, w, w_layer, *, act, out_dtype, tm, tn, name):
    m, k = x.shape
    n = w.shape[-1]
    blocks = (_nbytes((tm, k), F32) + _nbytes((k, tn), F32) + _nbytes((tm, tn), out_dtype))
    temps = _nbytes((tm, k), BF16) + _nbytes((k, tn), BF16) + 2 * _nbytes((tm, tn), F32)
    return pl.pallas_call(
        functools.partial(_norm_matmul_kernel, act=act),
        grid=(m // tm, n // tn),
        in_specs=[
            pl.BlockSpec((tm, k), lambda i, j: (i, 0)),
            pl.BlockSpec((None, 1, k), lambda i, j: (g_layer, 0, 0)),
            pl.BlockSpec((None, k, tn), lambda i, j: (w_layer, 0, j)),
        ],
        out_specs=pl.BlockSpec((tm, tn), lambda i, j: (i, j)),
        out_shape=jax.ShapeDtypeStruct((m, n), out_dtype),
        scratch_shapes=[pltpu.VMEM((tm, k), BF16)],
        compiler_params=pltpu.CompilerParams(
            dimension_semantics=("arbitrary", "arbitrary"),
            vmem_limit_bytes=_vmem_limit(blocks, temps)),
        name=name,
    )(x, g, w)


def _matmul_norm_res_kernel(a_ref, w_ref, x_ref, g_ref, o_ref, *, n_chunk):
    k = pl.program_id(1)

    @pl.when(k == 0)
    def _():
        o_ref[...] = jnp.zeros_like(o_ref)

    a = a_ref[...]
    for c in range(0, o_ref.shape[1], n_chunk):
        cols = pl.ds(c, n_chunk)
        o_ref[:, cols] += jnp.dot(a, w_ref[:, cols].astype(BF16), preferred_element_type=F32)

    @pl.when(k == pl.num_programs(1) - 1)
    def _():
        o_ref[...] = x_ref[...] + _rms_scale(o_ref[...], g_ref[...])


def _matmul_norm_res(a, w, w_layer, x, g, g_layer, *, tm, tk, name):
    m, kdim = a.shape
    n = w.shape[-1]
    n_chunk = min(n, 512)
    blocks = (_nbytes((tm, tk), BF16) + _nbytes((tk, n), F32) + _nbytes((tm, n), F32))
    temps = (_nbytes((tm, n), F32)
             + _nbytes((tk, n_chunk), BF16) + 2 * _nbytes((tm, n_chunk), F32) + _nbytes((tm, n), F32))
    return pl.pallas_call(
        functools.partial(_matmul_norm_res_kernel, n_chunk=n_chunk),
        grid=(m // tm, kdim // tk),
        in_specs=[
            pl.BlockSpec((tm, tk), lambda i, k: (i, k)),
            pl.BlockSpec((None, tk, n), lambda i, k: (w_layer, k, 0)),
            pl.BlockSpec((tm, n), lambda i, k: (i, 0), pipeline_mode=pl.Buffered(1)),
            pl.BlockSpec((None, 1, n), lambda i, k: (g_layer, 0, 0)),
        ],
        out_specs=pl.BlockSpec((tm, n), lambda i, k: (i, 0)),
        out_shape=jax.ShapeDtypeStruct((m, n), F32),
        compiler_params=pltpu.CompilerParams(
            dimension_semantics=("arbitrary", "arbitrary"),
            vmem_limit_bytes=_vmem_limit(blocks, temps)),
        name=name,
    )(a, w, x, g)


def _rope_tables(positions):
    half = HEAD_DIM // 2
    inv_freq = ROPE_THETA ** (-jnp.arange(half, dtype=F32) / half)
    ang = positions.astype(F32)[:, None] * inv_freq[None, :]
    cos = jnp.cos(ang)
    sin = jnp.sin(ang)
    reps = LANES // HEAD_DIM
    cos_t = jnp.tile(jnp.concatenate([cos, cos], axis=1), (1, reps))
    sin_t = jnp.tile(jnp.concatenate([-sin, sin], axis=1), (1, reps))
    return cos_t, sin_t


def _rope(x, cos, sin):
    half = HEAD_DIM // 2
    w = x.shape[1]
    lane = lax.broadcasted_iota(jnp.int32, x.shape, 1)
    first_half = (lane % HEAD_DIM) < half
    rot = jnp.where(first_half, pltpu.roll(x, w - half, axis=1), pltpu.roll(x, half, axis=1))
    return x * cos + rot * sin


def _swa_conv_kernel(p_ref, cos_ref, sin_ref, sink_ref, cw_ref,
                     cat_ref, kwin_ref, vwin_ref, ztail_ref,
                     kband_ref, vband_ref, zprev_ref, *, n_heads, n_kv):
    n = pl.program_id(1)
    blk = p_ref.shape[0]
    q_w = n_heads * HEAD_DIM
    kv_w = n_kv * HEAD_DIM
    conv_dim = cw_ref.shape[1]
    gqa = n_heads // n_kv

    @pl.when(n == 0)
    def _():
        kband_ref[...] = jnp.zeros_like(kband_ref)
        vband_ref[...] = jnp.zeros_like(vband_ref)
        zprev_ref[...] = jnp.zeros_like(zprev_ref)

    cos = cos_ref[...]
    sin = sin_ref[...]
    k_rot = _rope(p_ref[:, q_w:q_w + kv_w].astype(F32),
                  jnp.tile(cos, (1, kv_w // LANES)), jnp.tile(sin, (1, kv_w // LANES)))
    v_cur = p_ref[:, q_w + kv_w:q_w + 2 * kv_w]
    kband_ref[blk:, :] = k_rot.astype(BF16)
    vband_ref[blk:, :] = v_cur

    @pl.when(n == pl.num_programs(1) - 1)
    def _():
        kwin_ref[...] = k_rot
        vwin_ref[...] = v_cur.astype(F32)

    row = lax.broadcasted_iota(jnp.int32, (blk, 2 * blk), 0)
    col = lax.broadcasted_iota(jnp.int32, (blk, 2 * blk), 1)
    mask = (col > row) & (col <= row + blk) & ((col >= blk) | (n > 0))
    scale = HEAD_DIM ** -0.5

    for pair in range(q_w // LANES):
        lanes = slice(pair * LANES, (pair + 1) * LANES)
        q_rot = _rope(p_ref[:, lanes].astype(F32), cos, sin).astype(BF16)
        outs = []
        for sub in range(LANES // HEAD_DIM):
            h = pair * (LANES // HEAD_DIM) + sub
            kv = h // gqa
            kv_lanes = slice(kv * HEAD_DIM, (kv + 1) * HEAD_DIM)
            q_h = q_rot[:, sub * HEAD_DIM:(sub + 1) * HEAD_DIM]
            s = lax.dot_general(q_h, kband_ref[:, kv_lanes], (((1,), (1,)), ((), ())),
                                preferred_element_type=F32) * scale
            s = jnp.where(mask, s, -jnp.inf)
            sink = sink_ref[h]
            m = jnp.maximum(jnp.max(s, axis=-1, keepdims=True), sink)
            p = jnp.exp(s - m)
            denom = jnp.sum(p, axis=-1, keepdims=True) + jnp.exp(sink - m)
            o = jnp.dot(p.astype(BF16), vband_ref[:, kv_lanes], preferred_element_type=F32)
            outs.append(o / denom)
        cat_ref[:, lanes] = jnp.concatenate(outs, axis=1).astype(BF16)

    kband_ref[:blk, :] = kband_ref[blk:, :]
    vband_ref[:blk, :] = vband_ref[blk:, :]

    g_off = q_w + 2 * kv_w
    gate_b = p_ref[:, g_off:g_off + conv_dim].astype(F32)
    gate_c = p_ref[:, g_off + conv_dim:g_off + 2 * conv_dim].astype(F32)
    h_conv = p_ref[:, g_off + 2 * conv_dim:g_off + 3 * conv_dim].astype(F32)
    z = gate_c * h_conv
    zrow = lax.broadcasted_iota(jnp.int32, z.shape, 0)
    prev1 = jnp.broadcast_to(zprev_ref[SUBLANES - 1:SUBLANES, :], z.shape)
    prev2 = jnp.broadcast_to(zprev_ref[SUBLANES - 2:SUBLANES - 1, :], z.shape)
    z_m1 = jnp.where(zrow == 0, prev1, pltpu.roll(z, 1, axis=0))
    z_m2 = jnp.where(zrow == 0, prev2, jnp.where(zrow == 1, prev1, pltpu.roll(z, 2, axis=0)))
    conv = cw_ref[0:1, :] * z_m2
    conv = conv + cw_ref[1:2, :] * z_m1
    conv = conv + cw_ref[2:3, :] * z
    cat_ref[:, q_w:] = (gate_b * conv).astype(BF16)
    z_tail = z[blk - SUBLANES:, :]
    zprev_ref[...] = z_tail

    @pl.when(n == pl.num_programs(1) - 1)
    def _():
        ztail_ref[...] = z_tail


def _swa_conv(p, cos, sin, sinks, conv_w, layer, *, blk, n_heads, n_kv):
    b, s, pw = p.shape
    q_w = n_heads * HEAD_DIM
    kv_w = n_kv * HEAD_DIM
    conv_dim = conv_w.shape[-1]
    blocks = _nbytes((blk, pw), BF16) + _nbytes((blk, q_w + conv_dim), BF16) + (1 << 20)
    return pl.pallas_call(
        functools.partial(_swa_conv_kernel, n_heads=n_heads, n_kv=n_kv),
        grid=(b, s // blk),
        in_specs=[
            pl.BlockSpec((None, blk, pw), lambda bi, n: (bi, n, 0)),
            pl.BlockSpec((blk, LANES), lambda bi, n: (n, 0)),
            pl.BlockSpec((blk, LANES), lambda bi, n: (n, 0)),
            pl.BlockSpec(memory_space=pltpu.SMEM),
            pl.BlockSpec((None, CONV_WIDTH, conv_dim), lambda bi, n: (layer, 0, 0)),
        ],
        out_specs=[
            pl.BlockSpec((None, blk, q_w + conv_dim), lambda bi, n: (bi, n, 0)),
            pl.BlockSpec((None, blk, kv_w), lambda bi, n: (bi, 0, 0)),
            pl.BlockSpec((None, blk, kv_w), lambda bi, n: (bi, 0, 0)),
            pl.BlockSpec((None, SUBLANES, conv_dim), lambda bi, n: (bi, 0, 0)),
        ],
        out_shape=[
            jax.ShapeDtypeStruct((b, s, q_w + conv_dim), BF16),
            jax.ShapeDtypeStruct((b, blk, kv_w), F32),
            jax.ShapeDtypeStruct((b, blk, kv_w), F32),
            jax.ShapeDtypeStruct((b, SUBLANES, conv_dim), F32),
        ],
        scratch_shapes=[
            pltpu.VMEM((2 * blk, kv_w), BF16),
            pltpu.VMEM((2 * blk, kv_w), BF16),
            pltpu.VMEM((SUBLANES, conv_dim), F32),
        ],
        compiler_params=pltpu.CompilerParams(
            dimension_semantics=("arbitrary", "arbitrary"),
            vmem_limit_bytes=_vmem_limit(blocks, 16 << 20)),
        name="swa_conv",
    )(p, cos, sin, sinks, conv_w)


def _swa_conv_step_kernel(q_ref, kv_ref, gates_ref, ck_ref, cv_ref, st_ref, cos_ref, sin_ref, sink_ref, cw_ref,
                          attn_ref, conv_ref, kout_ref, vout_ref, stout_ref, *, n_heads, n_kv):
    nb, window, kv_w = ck_ref.shape
    gqa = n_heads // n_kv
    row_h = lax.broadcasted_iota(jnp.int32, (n_heads, kv_w), 0)
    lane_h = lax.broadcasted_iota(jnp.int32, (n_heads, kv_w), 1)
    own = (lane_h // HEAD_DIM) == (row_h // gqa)
    key_pos = lax.broadcasted_iota(jnp.int32, (n_heads, window), 1)
    cos = cos_ref[...]
    sin = sin_ref[...]
    sink = sink_ref[...]
    scale = HEAD_DIM ** -0.5
    for i in range(nb):
        q = jnp.where(own, _rope(q_ref[i], cos, sin), 0.0)
        k_new = _rope(kv_ref[i, 0:1, :], cos, sin)
        v_new = kv_ref[i, 1:2, :]
        ck = ck_ref[i]
        cv = cv_ref[i]
        s = lax.dot_general(q.astype(BF16), ck.astype(BF16), (((1,), (1,)), ((), ())),
                            preferred_element_type=F32) * scale
        s = jnp.where(key_pos >= 1, s, -jnp.inf)
        s_new = jnp.sum(q * k_new, axis=-1, keepdims=True) * scale
        m = jnp.maximum(jnp.maximum(jnp.max(s, axis=-1, keepdims=True), s_new), sink)
        p = jnp.exp(s - m)
        p_new = jnp.exp(s_new - m)
        denom = jnp.sum(p, axis=-1, keepdims=True) + p_new + jnp.exp(sink - m)
        o = jnp.dot(p.astype(BF16), cv.astype(BF16), preferred_element_type=F32)
        o = o + p_new * v_new
        o = jnp.where(own, o / denom, 0.0)
        folded = o[:, :LANES]
        for c in range(LANES, kv_w, LANES):
            folded = folded + o[:, c:c + LANES]
        for shift in range(HEAD_DIM, LANES, HEAD_DIM):
            folded = folded + pltpu.roll(folded, shift, axis=1)
        attn_ref[i] = folded

        kout_ref[i] = jnp.where(lax.broadcasted_iota(jnp.int32, ck.shape, 0) == window - 1,
                                k_new, pltpu.roll(ck, window - 1, axis=0))
        vout_ref[i] = jnp.where(lax.broadcasted_iota(jnp.int32, cv.shape, 0) == window - 1,
                                v_new, pltpu.roll(cv, window - 1, axis=0))

        gate_b = gates_ref[i, 0:1, :]
        z = gates_ref[i, 1:2, :] * gates_ref[i, 2:3, :]
        conv = cw_ref[0:1, :] * st_ref[i, 0:1, :]
        conv = conv + cw_ref[1:2, :] * st_ref[i, 1:2, :]
        conv = conv + cw_ref[2:3, :] * z
        conv_ref[i] = gate_b * conv
        stout_ref[i, 0:1, :] = st_ref[i, 1:2, :]
        stout_ref[i, 1:2, :] = z


def _swa_conv_step(q_rep, kv_new, gates, cache_k, cache_v, state, cos, sin, sinks, conv_w, layer, *,
                   nb, n_heads, n_kv):
    db, window, kv_w = cache_k.shape[1:]
    conv_dim = conv_w.shape[-1]
    seq = lambda i: (i, 0, 0)
    lay = lambda i: (layer, i, 0, 0)
    return pl.pallas_call(
        functools.partial(_swa_conv_step_kernel, n_heads=n_heads, n_kv=n_kv),
        grid=(db // nb,),
        in_specs=[
            pl.BlockSpec((nb, n_heads, kv_w), seq),
            pl.BlockSpec((nb, 2, kv_w), seq),
            pl.BlockSpec((nb, 3, conv_dim), seq),
            pl.BlockSpec((None, nb, window, kv_w), lay),
            pl.BlockSpec((None, nb, window, kv_w), lay),
            pl.BlockSpec((None, nb, CONV_WIDTH - 1, conv_dim), lay),
            pl.BlockSpec((1, kv_w), lambda i: (0, 0)),
            pl.BlockSpec((1, kv_w), lambda i: (0, 0)),
            pl.BlockSpec((None, n_heads, 1), lambda i: (layer, 0, 0)),
            pl.BlockSpec((None, CONV_WIDTH, conv_dim), lambda i: (layer, 0, 0)),
        ],
        out_specs=[
            pl.BlockSpec((nb, n_heads, LANES), seq),
            pl.BlockSpec((nb, 1, conv_dim), seq),
            pl.BlockSpec((nb, window, kv_w), seq),
            pl.BlockSpec((nb, window, kv_w), seq),
            pl.BlockSpec((nb, CONV_WIDTH - 1, conv_dim), seq),
        ],
        out_shape=[
            jax.ShapeDtypeStruct((db, n_heads, LANES), F32),
            jax.ShapeDtypeStruct((db, 1, conv_dim), F32),
            jax.ShapeDtypeStruct((db, window, kv_w), F32),
            jax.ShapeDtypeStruct((db, window, kv_w), F32),
            jax.ShapeDtypeStruct((db, CONV_WIDTH - 1, conv_dim), F32),
        ],
        compiler_params=pltpu.CompilerParams(dimension_semantics=("arbitrary",)),
        name="swa_conv_step",
    )(q_rep, kv_new, gates, cache_k, cache_v, state, cos, sin, sinks, conv_w)


def _layernorm(v, g, b):
    vc = v - jnp.mean(v, axis=-1, keepdims=True)
    var = jnp.mean(vc * vc, axis=-1, keepdims=True)
    return vc * lax.rsqrt(var + EPS) * g + b


def _cmlp_gate_kernel(z_ref, lg_ref, lb_ref, ws_ref, bs_ref, o_ref, vlast_ref):
    chunk = z_ref.shape[0]
    width = o_ref.shape[1]
    groups = ws_ref.shape[0]
    dg = width // groups
    vn = _layernorm(z_ref[:, width:].astype(F32), lg_ref[...], lb_ref[...])

    @pl.when(pl.program_id(1) == pl.num_programs(1) - 1)
    def _():
        vlast_ref[...] = vn

    vn_b = vn.astype(BF16)
    row = lax.broadcasted_iota(jnp.int32, (chunk, chunk), 0)
    col = lax.broadcasted_iota(jnp.int32, (chunk, chunk), 1)
    causal = row >= col
    for g in range(groups):
        lanes = slice(g * dg, (g + 1) * dg)
        w_tril = jnp.where(causal, ws_ref[g], 0.0).astype(BF16)
        mix = jnp.dot(w_tril, vn_b[:, lanes], preferred_element_type=F32) + bs_ref[g]
        o_ref[:, lanes] = (z_ref[:, lanes].astype(F32) * mix).astype(BF16)


def _cmlp_gate(z, ln_g, ln_b, w_s, b_s, layer, *, chunk):
    b, s, zw = z.shape
    width = zw // 2
    groups = w_s.shape[1]
    blocks = _nbytes((chunk, zw), BF16) + _nbytes((chunk, width), BF16) + _nbytes((chunk, width), F32) \
        + 2 * _nbytes((groups, chunk, chunk), F32)
    return pl.pallas_call(
        _cmlp_gate_kernel,
        grid=(b, s // chunk),
        in_specs=[
            pl.BlockSpec((None, chunk, zw), lambda bi, n: (bi, n, 0)),
            pl.BlockSpec((None, 1, width), lambda bi, n: (layer, 0, 0)),
            pl.BlockSpec((None, 1, width), lambda bi, n: (layer, 0, 0)),
            pl.BlockSpec((None, groups, chunk, chunk), lambda bi, n: (layer, 0, 0, 0)),
            pl.BlockSpec((None, groups, chunk, 1), lambda bi, n: (layer, 0, 0, 0)),
        ],
        out_specs=[
            pl.BlockSpec((None, chunk, width), lambda bi, n: (bi, n, 0)),
            pl.BlockSpec((None, chunk, width), lambda bi, n: (bi, 0, 0)),
        ],
        out_shape=[
            jax.ShapeDtypeStruct((b, s, width), BF16),
            jax.ShapeDtypeStruct((b, chunk, width), F32),
        ],
        compiler_params=pltpu.CompilerParams(
            dimension_semantics=("arbitrary", "arbitrary"),
            vmem_limit_bytes=_vmem_limit(blocks, 8 << 20)),
        name="cmlp_gate",
    )(z, ln_g, ln_b, w_s, b_s)


def _cmlp_gate_step_kernel(z_ref, lg_ref, lb_ref, scale_ref, bias_ref, o_ref, v_ref):
    width = o_ref.shape[1]
    vn = _layernorm(z_ref[:, width:], lg_ref[...], lb_ref[...])
    v_ref[...] = vn
    mix = scale_ref[...] * vn + bias_ref[...]
    o_ref[...] = (z_ref[:, :width] * mix).astype(BF16)


def _cmlp_gate_step(z, ln_g, ln_b, scale, bias, layer):
    rows, zw = z.shape
    width = zw // 2
    per_layer = pl.BlockSpec((None, 1, width), lambda i: (layer, 0, 0))
    return pl.pallas_call(
        _cmlp_gate_step_kernel,
        grid=(1,),
        in_specs=[pl.BlockSpec((rows, zw), lambda i: (0, 0)), per_layer, per_layer, per_layer, per_layer],
        out_specs=[pl.BlockSpec((rows, width), lambda i: (0, 0)), pl.BlockSpec((rows, width), lambda i: (0, 0))],
        out_shape=[jax.ShapeDtypeStruct((rows, width), BF16), jax.ShapeDtypeStruct((rows, width), F32)],
        compiler_params=pltpu.CompilerParams(dimension_semantics=("arbitrary",)),
        name="cmlp_gate_step",
    )(z, ln_g, ln_b, scale, bias)


def _ffn(x, g_pre, w_up, w_down, g_post, layer, *, tm, tag):
    hidden = _norm_matmul(x, g_pre, layer, w_up, layer, act=_relu_sq, out_dtype=BF16, tm=tm, tn=1024,
                          name=f"ffn_up_{tag}")
    return _matmul_norm_res(hidden, w_down, layer, x, g_post, layer, tm=tm, tk=512, name=f"ffn_down_{tag}")


def kernel(x_prompt, x_sample, cache_k_win, cache_v_win, state_conv, w_in_even, w_out_even, conv_w, attn_sinks,
           w_in_cmlp, w_out_cmlp, ln_v_g, ln_v_b, w_spatial, b_spatial, w_ffn_up, w_ffn_down,
           g_mix_pre, g_mix_post, g_ffn_pre, g_ffn_post):
    batch, seq, d_model = x_prompt.shape
    dec_batch, dec_seq, _ = x_sample.shape
    assert dec_seq == 1, "the decode kernels take one new token per sequence"
    depth = g_mix_pre.shape[0]
    n_even, _, window, n_kv, head_dim = cache_k_win.shape
    assert head_dim == HEAD_DIM
    n_heads = attn_sinks.shape[1]
    q_w = n_heads * HEAD_DIM
    kv_w = n_kv * HEAD_DIM
    conv_dim = conv_w.shape[-1]
    n_odd, groups, chunk, _ = w_spatial.shape
    cmlp_w = w_out_cmlp.shape[1]

    xp = x_prompt.reshape(batch * seq, d_model)
    xs = x_sample.reshape(dec_batch, d_model)
    tm_p, tm_s = ROW_TILE, dec_batch

    gains = [g.reshape(depth, 1, d_model) for g in (g_mix_pre, g_mix_post, g_ffn_pre, g_ffn_post)]
    g_mix_pre3, g_mix_post3, g_ffn_pre3, g_ffn_post3 = gains
    ln_g3 = ln_v_g.reshape(n_odd, 1, cmlp_w)
    ln_b3 = ln_v_b.reshape(n_odd, 1, cmlp_w)
    b_s4 = b_spatial.reshape(n_odd, groups, chunk, 1)
    step_scale = jnp.repeat(w_spatial[:, :, 0, 0], cmlp_w // groups, axis=1).reshape(n_odd, 1, cmlp_w)
    step_bias = jnp.repeat(b_spatial[:, :, 0], cmlp_w // groups, axis=1).reshape(n_odd, 1, cmlp_w)
    sinks3 = attn_sinks.reshape(n_even, n_heads, 1)

    cos_p, sin_p = _rope_tables(jnp.arange(seq))
    cos_s, sin_s = _rope_tables(PAST_LEN + jnp.arange(dec_seq))
    cos_s = jnp.tile(cos_s, (1, kv_w // LANES))
    sin_s = jnp.tile(sin_s, (1, kv_w // LANES))
    cache_k = cache_k_win.reshape(n_even, dec_batch, window, kv_w)
    cache_v = cache_v_win.reshape(n_even, dec_batch, window, kv_w)

    kp, vp, cp, up = [], [], [], []
    ks, vs, cs, us = [], [], [], []
    for i in range(depth):
        j = i // 2
        if i % 2 == 0:
            proj_p = _norm_matmul(xp, g_mix_pre3, i, w_in_even, j, act=_identity, out_dtype=BF16,
                                  tm=tm_p, tn=768, name="even_in_prompt")
            proj_s = _norm_matmul(xs, g_mix_pre3, i, w_in_even, j, act=_identity, out_dtype=F32,
                                  tm=tm_s, tn=768, name="even_in_sample")
            cat_p, k_p, v_p, z_p = _swa_conv(proj_p.reshape(batch, seq, -1), cos_p, sin_p, attn_sinks[j],
                                             conv_w, j, blk=window, n_heads=n_heads, n_kv=n_kv)
            kp.append(k_p.reshape(batch, window, n_kv, HEAD_DIM))
            vp.append(v_p.reshape(batch, window, n_kv, HEAD_DIM))
            cp.append(z_p[:, SUBLANES - (CONV_WIDTH - 1):, :])

            q_rep = jnp.tile(proj_s[:, :q_w].reshape(dec_batch, n_heads, HEAD_DIM), (1, 1, n_kv))
            kv_new = proj_s[:, q_w:q_w + 2 * kv_w].reshape(dec_batch, 2, kv_w)
            gates = proj_s[:, q_w + 2 * kv_w:].reshape(dec_batch, 3, conv_dim)
            attn_s, conv_s, k_s, v_s, c_s = _swa_conv_step(
                q_rep, kv_new, gates, cache_k, cache_v, state_conv, cos_s, sin_s, sinks3, conv_w, j,
                nb=8, n_heads=n_heads, n_kv=n_kv)
            cat_s = jnp.concatenate([attn_s[:, :, :HEAD_DIM].reshape(dec_batch, q_w),
                                     conv_s.reshape(dec_batch, conv_dim)], axis=1).astype(BF16)
            ks.append(k_s.reshape(dec_batch, window, n_kv, HEAD_DIM))
            vs.append(v_s.reshape(dec_batch, window, n_kv, HEAD_DIM))
            cs.append(c_s)
            xp = _matmul_norm_res(cat_p.reshape(batch * seq, -1), w_out_even, j, xp, g_mix_post3, i,
                                  tm=tm_p, tk=512, name="even_out_prompt")
            xs = _matmul_norm_res(cat_s, w_out_even, j, xs, g_mix_post3, i,
                                  tm=tm_s, tk=512, name="even_out_sample")
        else:
            z_p = _norm_matmul(xp, g_mix_pre3, i, w_in_cmlp, j, act=_gelu_exact, out_dtype=BF16,
                               tm=tm_p, tn=1024, name="cmlp_in_prompt")
            z_s = _norm_matmul(xs, g_mix_pre3, i, w_in_cmlp, j, act=_gelu_exact, out_dtype=F32,
                               tm=tm_s, tn=1024, name="cmlp_in_sample")
            gated_p, v_p = _cmlp_gate(z_p.reshape(batch, seq, -1), ln_g3, ln_b3, w_spatial, b_s4, j, chunk=chunk)
            up.append(v_p)
            gated_s, v_s = _cmlp_gate_step(z_s, ln_g3, ln_b3, step_scale, step_bias, j)
            us.append(v_s.reshape(dec_batch, dec_seq, cmlp_w))
            xp = _matmul_norm_res(gated_p.reshape(batch * seq, -1), w_out_cmlp, j, xp, g_mix_post3, i,
                                  tm=tm_p, tk=512, name="cmlp_out_prompt")
            xs = _matmul_norm_res(gated_s, w_out_cmlp, j, xs, g_mix_post3, i,
                                  tm=tm_s, tk=512, name="cmlp_out_sample")
        xp = _ffn(xp, g_ffn_pre3, w_ffn_up, w_ffn_down, g_ffn_post3, i, tm=tm_p, tag="prompt")
        xs = _ffn(xs, g_ffn_pre3, w_ffn_up, w_ffn_down, g_ffn_post3, i, tm=tm_s, tag="sample")

    return (xp.reshape(batch, seq, d_model), xs.reshape(dec_batch, dec_seq, d_model),
            jnp.stack(kp), jnp.stack(vp), jnp.stack(cp), jnp.stack(up),
            jnp.stack(ks), jnp.stack(vs), jnp.stack(cs), jnp.stack(us))
```

```python
import functools
import math

import jax
import jax.numpy as jnp
from jax import lax
from jax.experimental import pallas as pl
from jax.experimental.pallas import tpu as pltpu

F32 = jnp.float32
BF16 = jnp.bfloat16

EPS = 1e-6
HEAD_DIM = 64
ROPE_THETA = 10000.0
PAST_LEN = 16384
CONV_WIDTH = 3

LANES = 128
SUBLANES = 8
BF16_ROWS = 16
MXU_COLS = 256
VMEM_BYTES_V7X = 64 * 1024 * 1024

TAIL_ROWS = 128
PANEL_ROW_TILES = 4
KSTREAM_ROW_TILES = 8
EPILOGUE_ROW_BLOCKS = 5


def _vmem_limit(block_bytes, temp_bytes):
    need = 2 * block_bytes + temp_bytes + (4 << 20)
    return int(min(need, VMEM_BYTES_V7X - (6 << 20)))


def _nbytes(shape, dtype):
    return math.prod(shape) * jnp.dtype(dtype).itemsize


def _rms_scale(x, g):
    ms = jnp.mean(x * x, axis=-1, keepdims=True)
    return x * lax.rsqrt(ms + EPS) * g


def _identity(y):
    return y


def _gelu_exact(y):
    return 0.5 * y * (1.0 + lax.erf(y * math.sqrt(0.5)))


def _relu_sq(y):
    r = jnp.maximum(y, 0.0)
    return r * r


def _rmsnorm_kernel(x_ref, g_ref, xo_ref, h_ref):
    x = x_ref[...]
    xo_ref[...] = x
    h_ref[...] = _rms_scale(x, g_ref[...]).astype(BF16)


def _rmsnorm_rows(x, g, g_layer, *, out_rows, tm):
    m, d = x.shape
    blocks = 2 * _nbytes((tm, d), F32) + _nbytes((tm, d), BF16)
    return pl.pallas_call(
        _rmsnorm_kernel,
        grid=(m // tm,),
        in_specs=[pl.BlockSpec((tm, d), lambda i: (i, 0)),
                  pl.BlockSpec((None, 1, d), lambda i: (g_layer, 0, 0))],
        out_specs=[pl.BlockSpec((tm, d), lambda i: (i, 0)), pl.BlockSpec((tm, d), lambda i: (i, 0))],
        out_shape=[jax.ShapeDtypeStruct((out_rows, d), F32), jax.ShapeDtypeStruct((out_rows, d), BF16)],
        compiler_params=pltpu.CompilerParams(
            dimension_semantics=("arbitrary",), vmem_limit_bytes=_vmem_limit(blocks, 2 * _nbytes((tm, d), F32))),
        name="rmsnorm_rows",
    )(x, g)


def _panel_matmul_kernel(h_ref, w_ref, o_ref, wb_ref, *, act):
    def step(cast_panel):
        for c in range(0, o_ref.shape[1], MXU_COLS):
            cols = slice(c, c + MXU_COLS)
            if cast_panel:
                wb_ref[:, cols] = w_ref[:, cols].astype(BF16)
            y = jnp.dot(h_ref[...], wb_ref[:, cols], preferred_element_type=F32)
            o_ref[:, cols] = act(y).astype(o_ref.dtype)

    @pl.when(pl.program_id(1) == 0)
    def _():
        step(True)

    @pl.when(pl.program_id(1) > 0)
    def _():
        step(False)


def _panel_matmul(h, w, w_layer, *, act, tn, name):
    m, k = h.shape
    n = w.shape[-1]
    tm = m // PANEL_ROW_TILES
    assert m % (PANEL_ROW_TILES * BF16_ROWS) == 0 and n % tn == 0 and tn % MXU_COLS == 0
    blocks = _nbytes((tm, k), BF16) + _nbytes((k, tn), F32) + _nbytes((tm, tn), BF16)
    temps = _nbytes((k, tn), BF16) + 3 * _nbytes((tm, MXU_COLS), F32)
    return pl.pallas_call(
        functools.partial(_panel_matmul_kernel, act=act),
        grid=(n // tn, PANEL_ROW_TILES),
        in_specs=[
            pl.BlockSpec((tm, k), lambda j, i: (i, 0)),
            pl.BlockSpec((None, k, tn), lambda j, i: (w_layer, 0, j)),
        ],
        out_specs=pl.BlockSpec((tm, tn), lambda j, i: (i, j)),
        out_shape=jax.ShapeDtypeStruct((m, n), BF16),
        scratch_shapes=[pltpu.VMEM((k, tn), BF16)],
        compiler_params=pltpu.CompilerParams(
            dimension_semantics=("arbitrary", "arbitrary"),
            vmem_limit_bytes=_vmem_limit(blocks, temps)),
        name=name,
    )(h, w)


def _kstream_matmul_kernel(a_ref, w_ref, xk_ref, gp_ref, gn_ref, o_ref, *rest, nk, emit_next):
    if emit_next:
        hn_ref, x_sc = rest
    else:
        (x_sc,) = rest
    k = pl.program_id(1)
    tm, n = o_ref.shape
    xw = n // nk
    for c in range(nk):
        @pl.when(k == c)
        def _(c=c):
            x_sc[:, c * xw:(c + 1) * xw] = xk_ref[...]

    def accumulate(rows, assign):
        a = a_ref[rows, :]
        for c in range(0, n, 2 * MXU_COLS):
            cols = slice(c, c + 2 * MXU_COLS)
            part = jnp.dot(a, w_ref[:, cols], preferred_element_type=F32)
            if assign:
                o_ref[rows, cols] = part
            else:
                o_ref[rows, cols] += part

    @pl.when(k == 0)
    def _():
        accumulate(slice(None), True)

    @pl.when((k > 0) & (k < nk - 1))
    def _():
        accumulate(slice(None), False)

    @pl.when(k == nk - 1)
    def _():
        rb = tm // EPILOGUE_ROW_BLOCKS
        for r in range(0, tm, rb):
            rows = slice(r, r + rb)
            accumulate(rows, False)
            y = x_sc[rows, :] + _rms_scale(o_ref[rows, :], gp_ref[...])
            o_ref[rows, :] = y
            if emit_next:
                hn_ref[rows, :] = _rms_scale(y, gn_ref[...]).astype(BF16)


def _kstream_matmul(a, w, w_layer, x, g_post, g_post_layer, g_next, g_next_layer, *, tk, emit_next, name):
    m, kdim = a.shape
    n = w.shape[-1]
    tm = m // KSTREAM_ROW_TILES
    nk = kdim // tk
    assert m % KSTREAM_ROW_TILES == 0 and kdim % tk == 0 and nk >= 2
    assert tm % (EPILOGUE_ROW_BLOCKS * BF16_ROWS) == 0 and n % (nk * LANES) == 0 and n % (2 * MXU_COLS) == 0
    blocks = (_nbytes((tm, tk), BF16) + _nbytes((tk, n), BF16) + _nbytes((tm, n // nk), F32)
              + _nbytes((tm, n), F32) + (_nbytes((tm, n), BF16) if emit_next else 0))
    temps = _nbytes((tm, n), F32) + 4 * _nbytes((tm // EPILOGUE_ROW_BLOCKS, n), F32)
    out_specs = [pl.BlockSpec((tm, n), lambda i, k: (i, 0))]
    out_shape = [jax.ShapeDtypeStruct((m, n), F32)]
    if emit_next:
        out_specs.append(pl.BlockSpec((tm, n), lambda i, k: (i, 0)))
        out_shape.append(jax.ShapeDtypeStruct((m, n), BF16))
    return pl.pallas_call(
        functools.partial(_kstream_matmul_kernel, nk=nk, emit_next=emit_next),
        grid=(KSTREAM_ROW_TILES, nk),
        in_specs=[
            pl.BlockSpec((tm, tk), lambda i, k: (i, k)),
            pl.BlockSpec((None, tk, n), lambda i, k: (w_layer, k, 0)),
            pl.BlockSpec((tm, n // nk), lambda i, k: (i, k)),
            pl.BlockSpec((None, 1, n), lambda i, k: (g_post_layer, 0, 0)),
            pl.BlockSpec((None, 1, n), lambda i, k: (g_next_layer, 0, 0)),
        ],
        out_specs=out_specs,
        out_shape=out_shape,
        scratch_shapes=[pltpu.VMEM((tm, n), F32)],
        compiler_params=pltpu.CompilerParams(
            dimension_semantics=("arbitrary", "arbitrary"),
            vmem_limit_bytes=_vmem_limit(blocks, temps)),
        name=name,
    )(a, w, x, g_post, g_next)


def _rope_tables(positions):
    half = HEAD_DIM // 2
    inv_freq = ROPE_THETA ** (-jnp.arange(half, dtype=F32) / half)
    ang = positions.astype(F32)[:, None] * inv_freq[None, :]
    cos = jnp.cos(ang)
    sin = jnp.sin(ang)
    reps = LANES // HEAD_DIM
    cos_t = jnp.tile(jnp.concatenate([cos, cos], axis=1), (1, reps))
    sin_t = jnp.tile(jnp.concatenate([-sin, sin], axis=1), (1, reps))
    return cos_t, sin_t


def _rope(x, cos, sin):
    half = HEAD_DIM // 2
    w = x.shape[1]
    lane = lax.broadcasted_iota(jnp.int32, x.shape, 1)
    first_half = (lane % HEAD_DIM) < half
    rot = jnp.where(first_half, pltpu.roll(x, w - half, axis=1), pltpu.roll(x, half, axis=1))
    return x * cos + rot * sin


def _swa_conv_kernel(p_ref, cos_ref, sin_ref, sink_ref, cw_ref,
                     cat_ref, kwin_ref, vwin_ref, ztail_ref,
                     kband_ref, vband_ref, zprev_ref, *, n_heads, n_kv):
    n = pl.program_id(1)
    blk = p_ref.shape[0]
    q_w = n_heads * HEAD_DIM
    kv_w = n_kv * HEAD_DIM
    conv_dim = cw_ref.shape[1]
    gqa = n_heads // n_kv

    @pl.when(n == 0)
    def _():
        kband_ref[...] = jnp.zeros_like(kband_ref)
        vband_ref[...] = jnp.zeros_like(vband_ref)
        zprev_ref[...] = jnp.zeros_like(zprev_ref)

    cos = cos_ref[...]
    sin = sin_ref[...]
    k_rot = _rope(p_ref[:, q_w:q_w + kv_w].astype(F32),
                  jnp.tile(cos, (1, kv_w // LANES)), jnp.tile(sin, (1, kv_w // LANES)))
    v_cur = p_ref[:, q_w + kv_w:q_w + 2 * kv_w]
    kband_ref[blk:, :] = k_rot.astype(BF16)
    vband_ref[blk:, :] = v_cur

    @pl.when(n == pl.num_programs(1) - 1)
    def _():
        kwin_ref[...] = k_rot
        vwin_ref[...] = v_cur.astype(F32)

    row = lax.broadcasted_iota(jnp.int32, (blk, 2 * blk), 0)
    col = lax.broadcasted_iota(jnp.int32, (blk, 2 * blk), 1)
    mask = (col > row) & (col <= row + blk) & ((col >= blk) | (n > 0))
    scale = HEAD_DIM ** -0.5

    for pair in range(q_w // LANES):
        lanes = slice(pair * LANES, (pair + 1) * LANES)
        q_rot = _rope(p_ref[:, lanes].astype(F32), cos, sin).astype(BF16)
        outs = []
        for sub in range(LANES // HEAD_DIM):
            h = pair * (LANES // HEAD_DIM) + sub
            kv = h // gqa
            kv_lanes = slice(kv * HEAD_DIM, (kv + 1) * HEAD_DIM)
            q_h = q_rot[:, sub * HEAD_DIM:(sub + 1) * HEAD_DIM]
            s = lax.dot_general(q_h, kband_ref[:, kv_lanes], (((1,), (1,)), ((), ())),
                                preferred_element_type=F32) * scale
            s = jnp.where(mask, s, -jnp.inf)
            sink = sink_ref[h]
            m = jnp.maximum(jnp.max(s, axis=-1, keepdims=True), sink)
            p = jnp.exp(s - m)
            denom = jnp.sum(p, axis=-1, keepdims=True) + jnp.exp(sink - m)
            o = jnp.dot(p.astype(BF16), vband_ref[:, kv_lanes], preferred_element_type=F32)
            outs.append(o / denom)
        cat_ref[:, lanes] = jnp.concatenate(outs, axis=1).astype(BF16)

    kband_ref[:blk, :] = kband_ref[blk:, :]
    vband_ref[:blk, :] = vband_ref[blk:, :]

    g_off = q_w + 2 * kv_w
    gate_b = p_ref[:, g_off:g_off + conv_dim].astype(F32)
    gate_c = p_ref[:, g_off + conv_dim:g_off + 2 * conv_dim].astype(F32)
    h_conv = p_ref[:, g_off + 2 * conv_dim:g_off + 3 * conv_dim].astype(F32)
    z = gate_c * h_conv
    zrow = lax.broadcasted_iota(jnp.int32, z.shape, 0)
    prev1 = jnp.broadcast_to(zprev_ref[SUBLANES - 1:SUBLANES, :], z.shape)
    prev2 = jnp.broadcast_to(zprev_ref[SUBLANES - 2:SUBLANES - 1, :], z.shape)
    z_m1 = jnp.where(zrow == 0, prev1, pltpu.roll(z, 1, axis=0))
    z_m2 = jnp.where(zrow == 0, prev2, jnp.where(zrow == 1, prev1, pltpu.roll(z, 2, axis=0)))
    conv = cw_ref[0:1, :] * z_m2
    conv = conv + cw_ref[1:2, :] * z_m1
    conv = conv + cw_ref[2:3, :] * z
    cat_ref[:, q_w:] = (gate_b * conv).astype(BF16)
    z_tail = z[blk - SUBLANES:, :]
    zprev_ref[...] = z_tail

    @pl.when(n == pl.num_programs(1) - 1)
    def _():
        ztail_ref[...] = z_tail


def _swa_conv(p, cos, sin, sinks, conv_w, layer, *, batch, blk, n_heads, n_kv):
    rows, pw = p.shape
    nb = cos.shape[0] // blk
    q_w = n_heads * HEAD_DIM
    kv_w = n_kv * HEAD_DIM
    conv_dim = conv_w.shape[-1]
    blocks = _nbytes((blk, pw), BF16) + _nbytes((blk, q_w + conv_dim), BF16) + (1 << 20)
    return pl.pallas_call(
        functools.partial(_swa_conv_kernel, n_heads=n_heads, n_kv=n_kv),
        grid=(batch, nb),
        in_specs=[
            pl.BlockSpec((blk, pw), lambda bi, n: (bi * nb + n, 0)),
            pl.BlockSpec((blk, LANES), lambda bi, n: (n, 0)),
            pl.BlockSpec((blk, LANES), lambda bi, n: (n, 0)),
            pl.BlockSpec(memory_space=pltpu.SMEM),
            pl.BlockSpec((None, CONV_WIDTH, conv_dim), lambda bi, n: (layer, 0, 0)),
        ],
        out_specs=[
            pl.BlockSpec((blk, q_w + conv_dim), lambda bi, n: (bi * nb + n, 0)),
            pl.BlockSpec((None, blk, kv_w), lambda bi, n: (bi, 0, 0)),
            pl.BlockSpec((None, blk, kv_w), lambda bi, n: (bi, 0, 0)),
            pl.BlockSpec((None, SUBLANES, conv_dim), lambda bi, n: (bi, 0, 0)),
        ],
        out_shape=[
            jax.ShapeDtypeStruct((rows, q_w + conv_dim), BF16),
            jax.ShapeDtypeStruct((batch, blk, kv_w), F32),
            jax.ShapeDtypeStruct((batch, blk, kv_w), F32),
            jax.ShapeDtypeStruct((batch, SUBLANES, conv_dim), F32),
        ],
        scratch_shapes=[
            pltpu.VMEM((2 * blk, kv_w), BF16),
            pltpu.VMEM((2 * blk, kv_w), BF16),
            pltpu.VMEM((SUBLANES, conv_dim), F32),
        ],
        compiler_params=pltpu.CompilerParams(
            dimension_semantics=("arbitrary", "arbitrary"),
            vmem_limit_bytes=_vmem_limit(blocks, 16 << 20)),
        name="swa_conv",
    )(p, cos, sin, sinks, conv_w)


def _swa_conv_step_kernel(q_ref, kv_ref, gates_ref, ck_ref, cv_ref, st_ref, cos_ref, sin_ref, sink_ref, cw_ref,
                          attn_ref, conv_ref, kout_ref, vout_ref, stout_ref, *, n_heads, n_kv):
    nb, window, kv_w = ck_ref.shape
    gqa = n_heads // n_kv
    row_h = lax.broadcasted_iota(jnp.int32, (n_heads, kv_w), 0)
    lane_h = lax.broadcasted_iota(jnp.int32, (n_heads, kv_w), 1)
    own = (lane_h // HEAD_DIM) == (row_h // gqa)
    key_pos = lax.broadcasted_iota(jnp.int32, (n_heads, window), 1)
    cos = cos_ref[...]
    sin = sin_ref[...]
    sink = sink_ref[...]
    scale = HEAD_DIM ** -0.5
    for i in range(nb):
        q = jnp.where(own, _rope(q_ref[i], cos, sin), 0.0)
        k_new = _rope(kv_ref[i, 0:1, :], cos, sin)
        v_new = kv_ref[i, 1:2, :]
        ck = ck_ref[i]
        cv = cv_ref[i]
        s = lax.dot_general(q.astype(BF16), ck.astype(BF16), (((1,), (1,)), ((), ())),
                            preferred_element_type=F32) * scale
        s = jnp.where(key_pos >= 1, s, -jnp.inf)
        s_new = jnp.sum(q * k_new, axis=-1, keepdims=True) * scale
        m = jnp.maximum(jnp.maximum(jnp.max(s, axis=-1, keepdims=True), s_new), sink)
        p = jnp.exp(s - m)
        p_new = jnp.exp(s_new - m)
        denom = jnp.sum(p, axis=-1, keepdims=True) + p_new + jnp.exp(sink - m)
        o = jnp.dot(p.astype(BF16), cv.astype(BF16), preferred_element_type=F32)
        o = o + p_new * v_new
        o = jnp.where(own, o / denom, 0.0)
        folded = o[:, :LANES]
        for c in range(LANES, kv_w, LANES):
            folded = folded + o[:, c:c + LANES]
        for shift in range(HEAD_DIM, LANES, HEAD_DIM):
            folded = folded + pltpu.roll(folded, shift, axis=1)
        attn_ref[i] = folded

        kout_ref[i] = jnp.where(lax.broadcasted_iota(jnp.int32, ck.shape, 0) == window - 1,
                                k_new, pltpu.roll(ck, window - 1, axis=0))
        vout_ref[i] = jnp.where(lax.broadcasted_iota(jnp.int32, cv.shape, 0) == window - 1,
                                v_new, pltpu.roll(cv, window - 1, axis=0))

        gate_b = gates_ref[i, 0:1, :]
        z = gates_ref[i, 1:2, :] * gates_ref[i, 2:3, :]
        conv = cw_ref[0:1, :] * st_ref[i, 0:1, :]
        conv = conv + cw_ref[1:2, :] * st_ref[i, 1:2, :]
        conv = conv + cw_ref[2:3, :] * z
        conv_ref[i] = gate_b * conv
        stout_ref[i, 0:1, :] = st_ref[i, 1:2, :]
        stout_ref[i, 1:2, :] = z


def _swa_conv_step(q_rep, kv_new, gates, cache_k, cache_v, state, cos, sin, sinks, conv_w, layer, *,
                   nb, n_heads, n_kv):
    db, window, kv_w = cache_k.shape[1:]
    conv_dim = conv_w.shape[-1]
    seq = lambda i: (i, 0, 0)
    lay = lambda i: (layer, i, 0, 0)
    return pl.pallas_call(
        functools.partial(_swa_conv_step_kernel, n_heads=n_heads, n_kv=n_kv),
        grid=(db // nb,),
        in_specs=[
            pl.BlockSpec((nb, n_heads, kv_w), seq),
            pl.BlockSpec((nb, 2, kv_w), seq),
            pl.BlockSpec((nb, 3, conv_dim), seq),
            pl.BlockSpec((None, nb, window, kv_w), lay),
            pl.BlockSpec((None, nb, window, kv_w), lay),
            pl.BlockSpec((None, nb, CONV_WIDTH - 1, conv_dim), lay),
            pl.BlockSpec((1, kv_w), lambda i: (0, 0)),
            pl.BlockSpec((1, kv_w), lambda i: (0, 0)),
            pl.BlockSpec((None, n_heads, 1), lambda i: (layer, 0, 0)),
            pl.BlockSpec((None, CONV_WIDTH, conv_dim), lambda i: (layer, 0, 0)),
        ],
        out_specs=[
            pl.BlockSpec((nb, n_heads, LANES), seq),
            pl.BlockSpec((nb, 1, conv_dim), seq),
            pl.BlockSpec((nb, window, kv_w), seq),
            pl.BlockSpec((nb, window, kv_w), seq),
            pl.BlockSpec((nb, CONV_WIDTH - 1, conv_dim), seq),
        ],
        out_shape=[
            jax.ShapeDtypeStruct((db, n_heads, LANES), F32),
            jax.ShapeDtypeStruct((db, 1, conv_dim), F32),
            jax.ShapeDtypeStruct((db, window, kv_w), F32),
            jax.ShapeDtypeStruct((db, window, kv_w), F32),
            jax.ShapeDtypeStruct((db, CONV_WIDTH - 1, conv_dim), F32),
        ],
        compiler_params=pltpu.CompilerParams(dimension_semantics=("arbitrary",)),
        name="swa_conv_step",
    )(q_rep, kv_new, gates, cache_k, cache_v, state, cos, sin, sinks, conv_w)


def _layernorm(v, g, b):
    vc = v - jnp.mean(v, axis=-1, keepdims=True)
    var = jnp.mean(vc * vc, axis=-1, keepdims=True)
    return vc * lax.rsqrt(var + EPS) * g + b


def _cmlp_gate_kernel(z_ref, lg_ref, lb_ref, ws_ref, bs_ref, o_ref, vlast_ref):
    chunk = z_ref.shape[0]
    width = o_ref.shape[1]
    groups = ws_ref.shape[0]
    dg = width // groups
    vn = _layernorm(z_ref[:, width:].astype(F32), lg_ref[...], lb_ref[...])

    @pl.when(pl.program_id(1) == pl.num_programs(1) - 1)
    def _():
        vlast_ref[...] = vn

    vn_b = vn.astype(BF16)
    row = lax.broadcasted_iota(jnp.int32, (chunk, chunk), 0)
    col = lax.broadcasted_iota(jnp.int32, (chunk, chunk), 1)
    causal = row >= col
    for g in range(groups):
        lanes = slice(g * dg, (g + 1) * dg)
        w_tril = jnp.where(causal, ws_ref[g], 0.0).astype(BF16)
        mix = jnp.dot(w_tril, vn_b[:, lanes], preferred_element_type=F32) + bs_ref[g]
        o_ref[:, lanes] = (z_ref[:, lanes].astype(F32) * mix).astype(BF16)


def _cmlp_gate(z, ln_g, ln_b, w_s, b_s, layer, *, batch, n_chunks):
    rows, zw = z.shape
    width = zw // 2
    groups, chunk = w_s.shape[1:3]
    blocks = _nbytes((chunk, zw), BF16) + _nbytes((chunk, width), BF16) + _nbytes((chunk, width), F32) \
        + 2 * _nbytes((groups, chunk, chunk), F32)
    return pl.pallas_call(
        _cmlp_gate_kernel,
        grid=(batch, n_chunks),
        in_specs=[
            pl.BlockSpec((chunk, zw), lambda bi, n: (bi * n_chunks + n, 0)),
            pl.BlockSpec((None, 1, width), lambda bi, n: (layer, 0, 0)),
            pl.BlockSpec((None, 1, width), lambda bi, n: (layer, 0, 0)),
            pl.BlockSpec((None, groups, chunk, chunk), lambda bi, n: (layer, 0, 0, 0)),
            pl.BlockSpec((None, groups, chunk, 1), lambda bi, n: (layer, 0, 0, 0)),
        ],
        out_specs=[
            pl.BlockSpec((chunk, width), lambda bi, n: (bi * n_chunks + n, 0)),
            pl.BlockSpec((None, chunk, width), lambda bi, n: (bi, 0, 0)),
        ],
        out_shape=[
            jax.ShapeDtypeStruct((rows, width), BF16),
            jax.ShapeDtypeStruct((batch, chunk, width), F32),
        ],
        compiler_params=pltpu.CompilerParams(
            dimension_semantics=("arbitrary", "arbitrary"),
            vmem_limit_bytes=_vmem_limit(blocks, 8 << 20)),
        name="cmlp_gate",
    )(z, ln_g, ln_b, w_s, b_s)


def _cmlp_gate_step_kernel(z_ref, lg_ref, lb_ref, scale_ref, bias_ref, o_ref, v_ref):
    width = o_ref.shape[1]
    vn = _layernorm(z_ref[:, width:], lg_ref[...], lb_ref[...])
    v_ref[...] = vn
    mix = scale_ref[...] * vn + bias_ref[...]
    o_ref[...] = (z_ref[:, :width] * mix).astype(BF16)


def _cmlp_gate_step(z, ln_g, ln_b, scale, bias, layer):
    rows, zw = z.shape
    width = zw // 2
    per_layer = pl.BlockSpec((None, 1, width), lambda i: (layer, 0, 0))
    return pl.pallas_call(
        _cmlp_gate_step_kernel,
        grid=(1,),
        in_specs=[pl.BlockSpec((rows, zw), lambda i: (0, 0)), per_layer, per_layer, per_layer, per_layer],
        out_specs=[pl.BlockSpec((rows, width), lambda i: (0, 0)), pl.BlockSpec((rows, width), lambda i: (0, 0))],
        out_shape=[jax.ShapeDtypeStruct((rows, width), BF16), jax.ShapeDtypeStruct((rows, width), F32)],
        compiler_params=pltpu.CompilerParams(dimension_semantics=("arbitrary",)),
        name="cmlp_gate_step",
    )(z, ln_g, ln_b, scale, bias)


def _set_tail_rows(stream, tail, start):
    pad = jnp.zeros((TAIL_ROWS - tail.shape[0], tail.shape[1]), stream.dtype)
    return lax.dynamic_update_slice(stream, jnp.concatenate([tail.astype(stream.dtype), pad], axis=0), (start, 0))


def kernel(x_prompt, x_sample, cache_k_win, cache_v_win, state_conv, w_in_even, w_out_even, conv_w, attn_sinks,
           w_in_cmlp, w_out_cmlp, ln_v_g, ln_v_b, w_spatial, b_spatial, w_ffn_up, w_ffn_down,
           g_mix_pre, g_mix_post, g_ffn_pre, g_ffn_post):
    batch, seq, d_model = x_prompt.shape
    dec_batch, dec_seq, _ = x_sample.shape
    assert dec_seq == 1, "the decode kernels take one new token per sequence"
    depth = g_mix_pre.shape[0]
    n_even, _, window, n_kv, head_dim = cache_k_win.shape
    assert head_dim == HEAD_DIM
    n_heads = attn_sinks.shape[1]
    q_w = n_heads * HEAD_DIM
    kv_w = n_kv * HEAD_DIM
    conv_dim = conv_w.shape[-1]
    n_odd, groups, chunk, _ = w_spatial.shape
    cmlp_w = w_out_cmlp.shape[1]
    n_prompt = batch * seq
    rows = n_prompt + TAIL_ROWS
    assert dec_batch <= TAIL_ROWS

    gains = [g.reshape(depth, 1, d_model) for g in (g_mix_pre, g_mix_post, g_ffn_pre, g_ffn_post)]
    g_mix_pre3, g_mix_post3, g_ffn_pre3, g_ffn_post3 = gains
    ln_g3 = ln_v_g.reshape(n_odd, 1, cmlp_w)
    ln_b3 = ln_v_b.reshape(n_odd, 1, cmlp_w)
    b_s4 = b_spatial.reshape(n_odd, groups, chunk, 1)
    step_scale = jnp.repeat(w_spatial[:, :, 0, 0], cmlp_w // groups, axis=1).reshape(n_odd, 1, cmlp_w)
    step_bias = jnp.repeat(b_spatial[:, :, 0], cmlp_w // groups, axis=1).reshape(n_odd, 1, cmlp_w)
    sinks3 = attn_sinks.reshape(n_even, n_heads, 1)
    w_out_even_b = w_out_even.astype(BF16)
    w_out_cmlp_b = w_out_cmlp.astype(BF16)
    w_ffn_down_b = w_ffn_down.astype(BF16)

    cos_p, sin_p = _rope_tables(jnp.arange(seq))
    cos_s, sin_s = _rope_tables(PAST_LEN + jnp.arange(dec_seq))
    cos_s = jnp.tile(cos_s, (1, kv_w // LANES))
    sin_s = jnp.tile(sin_s, (1, kv_w // LANES))
    cache_k = cache_k_win.reshape(n_even, dec_batch, window, kv_w)
    cache_v = cache_v_win.reshape(n_even, dec_batch, window, kv_w)

    x, h = _rmsnorm_rows(x_prompt.reshape(n_prompt, d_model), g_mix_pre3, 0, out_rows=rows, tm=1024)
    xs = x_sample.reshape(dec_batch, d_model)
    _, hs = _rmsnorm_rows(xs, g_mix_pre3, 0, out_rows=dec_batch, tm=dec_batch)
    x = _set_tail_rows(x, xs, n_prompt)
    h = _set_tail_rows(h, hs, n_prompt)

    kp, vp, cp, up = [], [], [], []
    ks, vs, cs, us = [], [], [], []
    for i in range(depth):
        j = i // 2
        if i % 2 == 0:
            proj = _panel_matmul(h, w_in_even, j, act=_identity, tn=768, name="even_in")
            cat, k_p, v_p, z_p = _swa_conv(proj, cos_p, sin_p, attn_sinks[j], conv_w, j,
                                           batch=batch, blk=window, n_heads=n_heads, n_kv=n_kv)
            kp.append(k_p.reshape(batch, window, n_kv, HEAD_DIM))
            vp.append(v_p.reshape(batch, window, n_kv, HEAD_DIM))
            cp.append(z_p[:, SUBLANES - (CONV_WIDTH - 1):, :])

            proj_s = proj[n_prompt:n_prompt + dec_batch].astype(F32)
            q_rep = jnp.tile(proj_s[:, :q_w].reshape(dec_batch, n_heads, HEAD_DIM), (1, 1, n_kv))
            kv_new = proj_s[:, q_w:q_w + 2 * kv_w].reshape(dec_batch, 2, kv_w)
            gates = proj_s[:, q_w + 2 * kv_w:].reshape(dec_batch, 3, conv_dim)
            attn_s, conv_s, k_s, v_s, c_s = _swa_conv_step(
                q_rep, kv_new, gates, cache_k, cache_v, state_conv, cos_s, sin_s, sinks3, conv_w, j,
                nb=8, n_heads=n_heads, n_kv=n_kv)
            cat_s = jnp.concatenate([attn_s[:, :, :HEAD_DIM].reshape(dec_batch, q_w),
                                     conv_s.reshape(dec_batch, conv_dim)], axis=1)
            ks.append(k_s.reshape(dec_batch, window, n_kv, HEAD_DIM))
            vs.append(v_s.reshape(dec_batch, window, n_kv, HEAD_DIM))
            cs.append(c_s)
            mixed = _set_tail_rows(cat, cat_s, n_prompt)
            x, h = _kstream_matmul(mixed, w_out_even_b, j, x, g_mix_post3, i, g_ffn_pre3, i,
                                   tk=512, emit_next=True, name="even_out")
        else:
            z = _panel_matmul(h, w_in_cmlp, j, act=_gelu_exact, tn=1024, name="cmlp_in")
            gated, v_p = _cmlp_gate(z, ln_g3, ln_b3, w_spatial, b_s4, j, batch=batch, n_chunks=seq // chunk)
            up.append(v_p)
            gated_s, v_s = _cmlp_gate_step(z[n_prompt:n_prompt + dec_batch].astype(F32), ln_g3, ln_b3,
                                           step_scale, step_bias, j)
            us.append(v_s.reshape(dec_batch, dec_seq, cmlp_w))
            mixed = _set_tail_rows(gated, gated_s, n_prompt)
            x, h = _kstream_matmul(mixed, w_out_cmlp_b, j, x, g_mix_post3, i, g_ffn_pre3, i,
                                   tk=512, emit_next=True, name="cmlp_out")
        hidden = _panel_matmul(h, w_ffn_up, i, act=_relu_sq, tn=1024, name="ffn_up")
        last = i == depth - 1
        outs = _kstream_matmul(hidden, w_ffn_down_b, i, x, g_ffn_post3, i, g_mix_pre3, min(i + 1, depth - 1),
                               tk=1024, emit_next=not last, name="ffn_down")
        x, h = (outs[0], None) if last else outs

    return (x[:n_prompt].reshape(batch, seq, d_model), x[n_prompt:n_prompt + dec_batch].reshape(dec_batch, dec_seq, d_model),
            jnp.stack(kp), jnp.stack(vp), jnp.stack(cp), jnp.stack(up),
            jnp.stack(ks), jnp.stack(vs), jnp.stack(cs), jnp.stack(us))
```

```python
import functools
import math

import jax
import jax.numpy as jnp
from jax import lax
from jax.experimental import pallas as pl
from jax.experimental.pallas import tpu as pltpu

F32 = jnp.float32
BF16 = jnp.bfloat16

EPS = 1e-6
HEAD_DIM = 64
ROPE_THETA = 10000.0
PAST_LEN = 16384
CONV_WIDTH = 3

LANES = 128
SUBLANES = 8
BF16_ROWS = 16
MXU_COLS = 256
VMEM_BYTES_V7X = 64 * 1024 * 1024

TAIL_ROWS = 128
PANEL_ROW_TILES = 4
KSTREAM_ROW_TILES = 8
RESIDENT_ROW_TILES = 13
EPILOGUE_ROW_BLOCKS = 5
LOG2_E = math.log2(math.e)


def _vmem_limit(block_bytes, temp_bytes):
    need = 2 * block_bytes + temp_bytes + (4 << 20)
    return int(min(need, VMEM_BYTES_V7X - (6 << 20)))


def _nbytes(shape, dtype):
    return math.prod(shape) * jnp.dtype(dtype).itemsize


def _rms_scale(x, g):
    ms = jnp.mean(x * x, axis=-1, keepdims=True)
    return x * lax.rsqrt(ms + EPS) * g


def _identity(y):
    return y


def _gelu_exact(y):
    return 0.5 * y * (1.0 + lax.erf(y * math.sqrt(0.5)))


def _relu_sq(y):
    r = jnp.maximum(y, 0.0)
    return r * r


def _rmsnorm_kernel(x_ref, g_ref, xo_ref, h_ref):
    x = x_ref[...]
    xo_ref[...] = x
    h_ref[...] = _rms_scale(x, g_ref[...]).astype(BF16)


def _rmsnorm_rows(x, g, g_layer, *, out_rows, tm):
    m, d = x.shape
    blocks = 2 * _nbytes((tm, d), F32) + _nbytes((tm, d), BF16)
    return pl.pallas_call(
        _rmsnorm_kernel,
        grid=(m // tm,),
        in_specs=[pl.BlockSpec((tm, d), lambda i: (i, 0)),
                  pl.BlockSpec((None, 1, d), lambda i: (g_layer, 0, 0))],
        out_specs=[pl.BlockSpec((tm, d), lambda i: (i, 0)), pl.BlockSpec((tm, d), lambda i: (i, 0))],
        out_shape=[jax.ShapeDtypeStruct((out_rows, d), F32), jax.ShapeDtypeStruct((out_rows, d), BF16)],
        compiler_params=pltpu.CompilerParams(
            dimension_semantics=("arbitrary",), vmem_limit_bytes=_vmem_limit(blocks, 2 * _nbytes((tm, d), F32))),
        name="rmsnorm_rows",
    )(x, g)


def _panel_matmul_kernel(h_ref, w_ref, side_ref, o_ref, side_o_ref, wb_ref, *, act):
    def step(cast_panel):
        side_o_ref[...] = side_ref[...].astype(BF16)
        for c in range(0, o_ref.shape[1], MXU_COLS):
            cols = slice(c, c + MXU_COLS)
            if cast_panel:
                wb_ref[:, cols] = w_ref[:, cols].astype(BF16)
            y = jnp.dot(h_ref[...], wb_ref[:, cols], preferred_element_type=F32)
            o_ref[:, cols] = act(y).astype(o_ref.dtype)

    @pl.when(pl.program_id(1) == 0)
    def _():
        step(True)

    @pl.when(pl.program_id(1) > 0)
    def _():
        step(False)


def _panel_matmul(h, w, w_layer, side, side_layer, *, act, tn, name):
    m, k = h.shape
    n = w.shape[-1]
    k2, n2 = side.shape[1:]
    tm = m // PANEL_ROW_TILES
    steps = (n // tn) * PANEL_ROW_TILES
    n_slabs = min(steps, k2 // LANES)
    slab = k2 // n_slabs
    assert m % (PANEL_ROW_TILES * BF16_ROWS) == 0 and n % tn == 0 and tn % MXU_COLS == 0
    assert k2 % n_slabs == 0 and slab % BF16_ROWS == 0
    blocks = (_nbytes((tm, k), BF16) + _nbytes((k, tn), F32) + _nbytes((tm, tn), BF16)
              + _nbytes((slab, n2), F32) + _nbytes((slab, n2), BF16))
    temps = _nbytes((k, tn), BF16) + 3 * _nbytes((tm, MXU_COLS), F32)
    slab_index = lambda j, i: jnp.minimum(j * PANEL_ROW_TILES + i, n_slabs - 1)
    return pl.pallas_call(
        functools.partial(_panel_matmul_kernel, act=act),
        grid=(n // tn, PANEL_ROW_TILES),
        in_specs=[
            pl.BlockSpec((tm, k), lambda j, i: (i, 0)),
            pl.BlockSpec((None, k, tn), lambda j, i: (w_layer, 0, j)),
            pl.BlockSpec((None, slab, n2), lambda j, i: (side_layer, slab_index(j, i), 0)),
        ],
        out_specs=[
            pl.BlockSpec((tm, tn), lambda j, i: (i, j)),
            pl.BlockSpec((slab, n2), lambda j, i: (slab_index(j, i), 0)),
        ],
        out_shape=[jax.ShapeDtypeStruct((m, n), BF16), jax.ShapeDtypeStruct((k2, n2), BF16)],
        scratch_shapes=[pltpu.VMEM((k, tn), BF16)],
        compiler_params=pltpu.CompilerParams(
            dimension_semantics=("arbitrary", "arbitrary"),
            vmem_limit_bytes=_vmem_limit(blocks, temps)),
        name=name,
    )(h, w, side)


def _kstream_matmul_kernel(a_ref, w_ref, xk_ref, gp_ref, gn_ref, o_ref, *rest, nk, emit_next):
    if emit_next:
        hn_ref, x_sc = rest
    else:
        (x_sc,) = rest
    k = pl.program_id(1)
    tm, n = o_ref.shape
    xw = n // nk
    for c in range(nk):
        @pl.when(k == c)
        def _(c=c):
            x_sc[:, c * xw:(c + 1) * xw] = xk_ref[...]

    def accumulate(rows, assign):
        a = a_ref[rows, :]
        for c in range(0, n, 2 * MXU_COLS):
            cols = slice(c, c + 2 * MXU_COLS)
            part = jnp.dot(a, w_ref[:, cols], preferred_element_type=F32)
            if assign:
                o_ref[rows, cols] = part
            else:
                o_ref[rows, cols] += part

    @pl.when(k == 0)
    def _():
        accumulate(slice(None), True)

    @pl.when((k > 0) & (k < nk - 1))
    def _():
        accumulate(slice(None), False)

    @pl.when(k == nk - 1)
    def _():
        rb = tm // EPILOGUE_ROW_BLOCKS
        for r in range(0, tm, rb):
            rows = slice(r, r + rb)
            accumulate(rows, False)
            y = x_sc[rows, :] + _rms_scale(o_ref[rows, :], gp_ref[...])
            o_ref[rows, :] = y
            if emit_next:
                hn_ref[rows, :] = _rms_scale(y, gn_ref[...]).astype(BF16)


def _kstream_matmul(a, w, x, g_post, g_post_layer, g_next, g_next_layer, *, tk, emit_next, name):
    m, kdim = a.shape
    n = w.shape[-1]
    tm = m // KSTREAM_ROW_TILES
    nk = kdim // tk
    assert m % KSTREAM_ROW_TILES == 0 and kdim % tk == 0 and nk >= 2
    assert tm % (EPILOGUE_ROW_BLOCKS * BF16_ROWS) == 0 and n % (nk * LANES) == 0 and n % (2 * MXU_COLS) == 0
    blocks = (_nbytes((tm, tk), BF16) + _nbytes((tk, n), BF16) + _nbytes((tm, n // nk), F32)
              + _nbytes((tm, n), F32) + (_nbytes((tm, n), BF16) if emit_next else 0))
    temps = _nbytes((tm, n), F32) + 4 * _nbytes((tm // EPILOGUE_ROW_BLOCKS, n), F32)
    out_specs = [pl.BlockSpec((tm, n), lambda i, k: (i, 0))]
    out_shape = [jax.ShapeDtypeStruct((m, n), F32)]
    if emit_next:
        out_specs.append(pl.BlockSpec((tm, n), lambda i, k: (i, 0)))
        out_shape.append(jax.ShapeDtypeStruct((m, n), BF16))
    return pl.pallas_call(
        functools.partial(_kstream_matmul_kernel, nk=nk, emit_next=emit_next),
        grid=(KSTREAM_ROW_TILES, nk),
        in_specs=[
            pl.BlockSpec((tm, tk), lambda i, k: (i, k)),
            pl.BlockSpec((tk, n), lambda i, k: (k, 0)),
            pl.BlockSpec((tm, n // nk), lambda i, k: (i, k)),
            pl.BlockSpec((None, 1, n), lambda i, k: (g_post_layer, 0, 0)),
            pl.BlockSpec((None, 1, n), lambda i, k: (g_next_layer, 0, 0)),
        ],
        out_specs=out_specs,
        out_shape=out_shape,
        scratch_shapes=[pltpu.VMEM((tm, n), F32)],
        compiler_params=pltpu.CompilerParams(
            dimension_semantics=("arbitrary", "arbitrary"),
            vmem_limit_bytes=_vmem_limit(blocks, temps)),
        name=name,
    )(a, w, x, g_post, g_next)


def _resident_matmul_kernel(a_ref, w_ref, x_ref, gp_ref, gn_ref, o_ref, hn_ref):
    tm, n = o_ref.shape
    rb = tm // EPILOGUE_ROW_BLOCKS
    for r in range(0, tm, rb):
        rows = slice(r, r + rb)
        a = a_ref[rows, :]
        for c in range(0, n, 2 * MXU_COLS):
            cols = slice(c, c + 2 * MXU_COLS)
            o_ref[rows, cols] = jnp.dot(a, w_ref[:, cols], preferred_element_type=F32)
        y = x_ref[rows, :] + _rms_scale(o_ref[rows, :], gp_ref[...])
        o_ref[rows, :] = y
        hn_ref[rows, :] = _rms_scale(y, gn_ref[...]).astype(BF16)


def _resident_matmul(a, w, x, g_post, g_post_layer, g_next, g_next_layer, *, name):
    m, kdim = a.shape
    n = w.shape[-1]
    tm = m // RESIDENT_ROW_TILES
    assert m % RESIDENT_ROW_TILES == 0 and tm % (EPILOGUE_ROW_BLOCKS * BF16_ROWS) == 0 and n % (2 * MXU_COLS) == 0
    blocks = _nbytes((tm, kdim), BF16) + 2 * _nbytes((tm, n), F32) + _nbytes((tm, n), BF16)
    temps = _nbytes((kdim, n), BF16) + 4 * _nbytes((tm // EPILOGUE_ROW_BLOCKS, n), F32)
    return pl.pallas_call(
        _resident_matmul_kernel,
        grid=(RESIDENT_ROW_TILES,),
        in_specs=[
            pl.BlockSpec((tm, kdim), lambda i: (i, 0)),
            pl.BlockSpec((kdim, n), lambda i: (0, 0), pipeline_mode=pl.Buffered(1)),
            pl.BlockSpec((tm, n), lambda i: (i, 0)),
            pl.BlockSpec((None, 1, n), lambda i: (g_post_layer, 0, 0)),
            pl.BlockSpec((None, 1, n), lambda i: (g_next_layer, 0, 0)),
        ],
        out_specs=[pl.BlockSpec((tm, n), lambda i: (i, 0)), pl.BlockSpec((tm, n), lambda i: (i, 0))],
        out_shape=[jax.ShapeDtypeStruct((m, n), F32), jax.ShapeDtypeStruct((m, n), BF16)],
        compiler_params=pltpu.CompilerParams(
            dimension_semantics=("arbitrary",), vmem_limit_bytes=_vmem_limit(blocks, temps)),
        name=name,
    )(a, w, x, g_post, g_next)


def _rope_tables(positions):
    half = HEAD_DIM // 2
    inv_freq = ROPE_THETA ** (-jnp.arange(half, dtype=F32) / half)
    ang = positions.astype(F32)[:, None] * inv_freq[None, :]
    cos = jnp.cos(ang)
    sin = jnp.sin(ang)
    reps = LANES // HEAD_DIM
    cos_t = jnp.tile(jnp.concatenate([cos, cos], axis=1), (1, reps))
    sin_t = jnp.tile(jnp.concatenate([-sin, sin], axis=1), (1, reps))
    return cos_t, sin_t


def _rope(x, cos, sin):
    half = HEAD_DIM // 2
    w = x.shape[1]
    lane = lax.broadcasted_iota(jnp.int32, x.shape, 1)
    first_half = (lane % HEAD_DIM) < half
    rot = jnp.where(first_half, pltpu.roll(x, w - half, axis=1), pltpu.roll(x, half, axis=1))
    return x * cos + rot * sin


def _swa_conv_kernel(p_ref, cosq_ref, sinq_ref, cos_ref, sin_ref, sink_ref, cw_ref,
                     cat_ref, kwin_ref, vwin_ref, ztail_ref,
                     kband_ref, vtband_ref, zprev_ref, *, n_heads, n_kv):
    n = pl.program_id(1)
    blk = p_ref.shape[0]
    q_w = n_heads * HEAD_DIM
    kv_w = n_kv * HEAD_DIM
    conv_dim = cw_ref.shape[1]
    gqa = n_heads // n_kv

    @pl.when(n == 0)
    def _():
        kband_ref[...] = jnp.zeros_like(kband_ref)
        vtband_ref[...] = jnp.zeros_like(vtband_ref)
        zprev_ref[...] = jnp.zeros_like(zprev_ref)

    k_rot = _rope(p_ref[:, q_w:q_w + kv_w].astype(F32),
                  jnp.tile(cos_ref[...], (1, kv_w // LANES)), jnp.tile(sin_ref[...], (1, kv_w // LANES)))
    v_cur = p_ref[:, q_w + kv_w:q_w + 2 * kv_w].astype(F32)
    kband_ref[blk:, :] = k_rot.astype(BF16)
    vtband_ref[:, blk:] = v_cur.T.astype(BF16)

    @pl.when(n == pl.num_programs(1) - 1)
    def _():
        kwin_ref[...] = k_rot
        vwin_ref[...] = v_cur

    key = lax.broadcasted_iota(jnp.int32, (blk, blk), 0)
    qry = lax.broadcasted_iota(jnp.int32, (blk, blk), 1)
    own = key <= qry
    no_prev = jnp.where(own | (n > 0), 0.0, -jnp.inf)
    cos_q = cosq_ref[...]
    sin_q = sinq_ref[...]
    k_heads = [kband_ref[:, kv * HEAD_DIM:(kv + 1) * HEAD_DIM] for kv in range(n_kv)]

    heads_per_group = LANES // HEAD_DIM
    s_bands = []
    for pair in range(q_w // LANES):
        q_rot = _rope(p_ref[:, pair * LANES:(pair + 1) * LANES].astype(F32), cos_q, sin_q).astype(BF16)
        for sub in range(heads_per_group):
            kv = (pair * heads_per_group + sub) // gqa
            q_h = q_rot[:, sub * HEAD_DIM:(sub + 1) * HEAD_DIM]
            s_bands.append(lax.dot_general(k_heads[kv], q_h, (((1,), (1,)), ((), ())),
                                           preferred_element_type=F32))
    p_bands, inv_denoms = [], []
    for h, s_band in enumerate(s_bands):
        s = jnp.where(own, s_band[blk:], s_band[:blk]) + no_prev
        sink = sink_ref[h] * LOG2_E
        m = jnp.maximum(jnp.max(s, axis=0, keepdims=True), sink)
        p = jnp.exp2(s - m)
        denom = jnp.sum(p, axis=0, keepdims=True) + jnp.exp2(sink - m)
        p_bands.append(jnp.concatenate([jnp.where(own, 0.0, p), jnp.where(own, p, 0.0)], axis=0).astype(BF16))
        inv_denoms.append(1.0 / denom)
    for pair in range(q_w // LANES):
        outs_t = []
        for sub in range(heads_per_group):
            h = pair * heads_per_group + sub
            kv = h // gqa
            o_t = jnp.dot(vtband_ref[kv * HEAD_DIM:(kv + 1) * HEAD_DIM, :], p_bands[h],
                          preferred_element_type=F32)
            outs_t.append(o_t * inv_denoms[h])
        cat_ref[:, pair * LANES:(pair + 1) * LANES] = jnp.concatenate(outs_t, axis=0).T.astype(BF16)

    kband_ref[:blk, :] = kband_ref[blk:, :]
    vtband_ref[:, :blk] = vtband_ref[:, blk:]

    g_off = q_w + 2 * kv_w
    gate_b = p_ref[:, g_off:g_off + conv_dim].astype(F32)
    gate_c = p_ref[:, g_off + conv_dim:g_off + 2 * conv_dim].astype(F32)
    h_conv = p_ref[:, g_off + 2 * conv_dim:g_off + 3 * conv_dim].astype(F32)
    z = gate_c * h_conv
    z_prev = zprev_ref[...]
    top = lax.broadcasted_iota(jnp.int32, z_prev.shape, 0)

    def delayed(d):
        r = pltpu.roll(z, d, axis=0)
        head = jnp.where(top < d, pltpu.roll(z_prev, d, axis=0), r[:SUBLANES])
        return jnp.concatenate([head, r[SUBLANES:]], axis=0)

    conv = cw_ref[0:1, :] * delayed(2)
    conv = conv + cw_ref[1:2, :] * delayed(1)
    conv = conv + cw_ref[2:3, :] * z
    cat_ref[:, q_w:] = (gate_b * conv).astype(BF16)
    z_tail = z[blk - SUBLANES:, :]
    zprev_ref[...] = z_tail

    @pl.when(n == pl.num_programs(1) - 1)
    def _():
        ztail_ref[...] = z_tail


def _swa_conv(p, cos, sin, sinks, conv_w, layer, *, batch, blk, n_heads, n_kv):
    rows, pw = p.shape
    nb = cos.shape[0] // blk
    q_scale = HEAD_DIM ** -0.5 * LOG2_E
    table = pl.BlockSpec((blk, LANES), lambda bi, n: (n, 0))
    q_w = n_heads * HEAD_DIM
    kv_w = n_kv * HEAD_DIM
    conv_dim = conv_w.shape[-1]
    blocks = _nbytes((blk, pw), BF16) + _nbytes((blk, q_w + conv_dim), BF16) + (1 << 20)
    return pl.pallas_call(
        functools.partial(_swa_conv_kernel, n_heads=n_heads, n_kv=n_kv),
        grid=(batch, nb),
        in_specs=[
            pl.BlockSpec((blk, pw), lambda bi, n: (bi * nb + n, 0)),
            table, table, table, table,
            pl.BlockSpec(memory_space=pltpu.SMEM),
            pl.BlockSpec((None, CONV_WIDTH, conv_dim), lambda bi, n: (layer, 0, 0)),
        ],
        out_specs=[
            pl.BlockSpec((blk, q_w + conv_dim), lambda bi, n: (bi * nb + n, 0)),
            pl.BlockSpec((None, blk, kv_w), lambda bi, n: (bi, 0, 0)),
            pl.BlockSpec((None, blk, kv_w), lambda bi, n: (bi, 0, 0)),
            pl.BlockSpec((None, SUBLANES, conv_dim), lambda bi, n: (bi, 0, 0)),
        ],
        out_shape=[
            jax.ShapeDtypeStruct((rows, q_w + conv_dim), BF16),
            jax.ShapeDtypeStruct((batch, blk, kv_w), F32),
            jax.ShapeDtypeStruct((batch, blk, kv_w), F32),
            jax.ShapeDtypeStruct((batch, SUBLANES, conv_dim), F32),
        ],
        scratch_shapes=[
            pltpu.VMEM((2 * blk, kv_w), BF16),
            pltpu.VMEM((kv_w, 2 * blk), BF16),
            pltpu.VMEM((SUBLANES, conv_dim), F32),
        ],
        compiler_params=pltpu.CompilerParams(
            dimension_semantics=("arbitrary", "arbitrary"),
            vmem_limit_bytes=_vmem_limit(blocks, 16 << 20)),
        name="swa_conv",
    )(p, cos * q_scale, sin * q_scale, cos, sin, sinks, conv_w)


def _swa_conv_step_kernel(q_ref, kv_ref, gates_ref, ck_ref, cv_ref, st_ref, cos_ref, sin_ref, sink_ref, cw_ref,
                          attn_ref, conv_ref, kout_ref, vout_ref, stout_ref, *, n_heads, n_kv):
    nb, window, kv_w = ck_ref.shape
    gqa = n_heads // n_kv
    row_h = lax.broadcasted_iota(jnp.int32, (n_heads, kv_w), 0)
    lane_h = lax.broadcasted_iota(jnp.int32, (n_heads, kv_w), 1)
    own = (lane_h // HEAD_DIM) == (row_h // gqa)
    key_pos = lax.broadcasted_iota(jnp.int32, (n_heads, window), 1)
    cos = cos_ref[...]
    sin = sin_ref[...]
    sink = sink_ref[...]
    scale = HEAD_DIM ** -0.5
    for i in range(nb):
        q = jnp.where(own, _rope(q_ref[i], cos, sin), 0.0)
        k_new = _rope(kv_ref[i, 0:1, :], cos, sin)
        v_new = kv_ref[i, 1:2, :]
        ck = ck_ref[i]
        cv = cv_ref[i]
        s = lax.dot_general(q.astype(BF16), ck.astype(BF16), (((1,), (1,)), ((), ())),
                            preferred_element_type=F32) * scale
        s = jnp.where(key_pos >= 1, s, -jnp.inf)
        s_new = jnp.sum(q * k_new, axis=-1, keepdims=True) * scale
        m = jnp.maximum(jnp.maximum(jnp.max(s, axis=-1, keepdims=True), s_new), sink)
        p = jnp.exp(s - m)
        p_new = jnp.exp(s_new - m)
        denom = jnp.sum(p, axis=-1, keepdims=True) + p_new + jnp.exp(sink - m)
        o = jnp.dot(p.astype(BF16), cv.astype(BF16), preferred_element_type=F32)
        o = o + p_new * v_new
        o = jnp.where(own, o / denom, 0.0)
        folded = o[:, :LANES]
        for c in range(LANES, kv_w, LANES):
            folded = folded + o[:, c:c + LANES]
        for shift in range(HEAD_DIM, LANES, HEAD_DIM):
            folded = folded + pltpu.roll(folded, shift, axis=1)
        attn_ref[i] = folded

        kout_ref[i] = jnp.where(lax.broadcasted_iota(jnp.int32, ck.shape, 0) == window - 1,
                                k_new, pltpu.roll(ck, window - 1, axis=0))
        vout_ref[i] = jnp.where(lax.broadcasted_iota(jnp.int32, cv.shape, 0) == window - 1,
                                v_new, pltpu.roll(cv, window - 1, axis=0))

        gate_b = gates_ref[i, 0:1, :]
        z = gates_ref[i, 1:2, :] * gates_ref[i, 2:3, :]
        conv = cw_ref[0:1, :] * st_ref[i, 0:1, :]
        conv = conv + cw_ref[1:2, :] * st_ref[i, 1:2, :]
        conv = conv + cw_ref[2:3, :] * z
        conv_ref[i] = gate_b * conv
        stout_ref[i, 0:1, :] = st_ref[i, 1:2, :]
        stout_ref[i, 1:2, :] = z


def _swa_conv_step(q_rep, kv_new, gates, cache_k, cache_v, state, cos, sin, sinks, conv_w, layer, *,
                   nb, n_heads, n_kv):
    db, window, kv_w = cache_k.shape[1:]
    conv_dim = conv_w.shape[-1]
    seq = lambda i: (i, 0, 0)
    lay = lambda i: (layer, i, 0, 0)
    return pl.pallas_call(
        functools.partial(_swa_conv_step_kernel, n_heads=n_heads, n_kv=n_kv),
        grid=(db // nb,),
        in_specs=[
            pl.BlockSpec((nb, n_heads, kv_w), seq),
            pl.BlockSpec((nb, 2, kv_w), seq),
            pl.BlockSpec((nb, 3, conv_dim), seq),
            pl.BlockSpec((None, nb, window, kv_w), lay),
            pl.BlockSpec((None, nb, window, kv_w), lay),
            pl.BlockSpec((None, nb, CONV_WIDTH - 1, conv_dim), lay),
            pl.BlockSpec((1, kv_w), lambda i: (0, 0)),
            pl.BlockSpec((1, kv_w), lambda i: (0, 0)),
            pl.BlockSpec((None, n_heads, 1), lambda i: (layer, 0, 0)),
            pl.BlockSpec((None, CONV_WIDTH, conv_dim), lambda i: (layer, 0, 0)),
        ],
        out_specs=[
            pl.BlockSpec((nb, n_heads, LANES), seq),
            pl.BlockSpec((nb, 1, conv_dim), seq),
            pl.BlockSpec((nb, window, kv_w), seq),
            pl.BlockSpec((nb, window, kv_w), seq),
            pl.BlockSpec((nb, CONV_WIDTH - 1, conv_dim), seq),
        ],
        out_shape=[
            jax.ShapeDtypeStruct((db, n_heads, LANES), F32),
            jax.ShapeDtypeStruct((db, 1, conv_dim), F32),
            jax.ShapeDtypeStruct((db, window, kv_w), F32),
            jax.ShapeDtypeStruct((db, window, kv_w), F32),
            jax.ShapeDtypeStruct((db, CONV_WIDTH - 1, conv_dim), F32),
        ],
        compiler_params=pltpu.CompilerParams(dimension_semantics=("arbitrary",)),
        name="swa_conv_step",
    )(q_rep, kv_new, gates, cache_k, cache_v, state, cos, sin, sinks, conv_w)


def _layernorm(v, g, b):
    vc = v - jnp.mean(v, axis=-1, keepdims=True)
    var = jnp.mean(vc * vc, axis=-1, keepdims=True)
    return vc * lax.rsqrt(var + EPS) * g + b


def _cmlp_gate_kernel(z_ref, lg_ref, lb_ref, ws_ref, bs_ref, o_ref, vlast_ref, wt_ref):
    chunk = z_ref.shape[0]
    width = o_ref.shape[1]
    groups = ws_ref.shape[0]
    dg = width // groups

    @pl.when((pl.program_id(0) == 0) & (pl.program_id(1) == 0))
    def _():
        row = lax.broadcasted_iota(jnp.int32, (chunk, chunk), 0)
        col = lax.broadcasted_iota(jnp.int32, (chunk, chunk), 1)
        for g in range(groups):
            wt_ref[g] = jnp.where(row >= col, ws_ref[g], 0.0).astype(BF16)

    vn = _layernorm(z_ref[:, width:].astype(F32), lg_ref[...], lb_ref[...])

    @pl.when(pl.program_id(1) == pl.num_programs(1) - 1)
    def _():
        vlast_ref[...] = vn

    vn_b = vn.astype(BF16)
    for g in range(groups):
        lanes = slice(g * dg, (g + 1) * dg)
        mix = jnp.dot(wt_ref[g], vn_b[:, lanes], preferred_element_type=F32) + bs_ref[g]
        o_ref[:, lanes] = (z_ref[:, lanes].astype(F32) * mix).astype(BF16)


def _cmlp_gate(z, ln_g, ln_b, w_s, b_s, layer, *, batch, n_chunks):
    rows, zw = z.shape
    width = zw // 2
    groups, chunk = w_s.shape[1:3]
    blocks = _nbytes((chunk, zw), BF16) + _nbytes((chunk, width), BF16) + _nbytes((chunk, width), F32) \
        + 2 * _nbytes((groups, chunk, chunk), F32)
    return pl.pallas_call(
        _cmlp_gate_kernel,
        grid=(batch, n_chunks),
        in_specs=[
            pl.BlockSpec((chunk, zw), lambda bi, n: (bi * n_chunks + n, 0)),
            pl.BlockSpec((None, 1, width), lambda bi, n: (layer, 0, 0)),
            pl.BlockSpec((None, 1, width), lambda bi, n: (layer, 0, 0)),
            pl.BlockSpec((None, groups, chunk, chunk), lambda bi, n: (layer, 0, 0, 0)),
            pl.BlockSpec((None, groups, chunk, 1), lambda bi, n: (layer, 0, 0, 0)),
        ],
        out_specs=[
            pl.BlockSpec((chunk, width), lambda bi, n: (bi * n_chunks + n, 0)),
            pl.BlockSpec((None, chunk, width), lambda bi, n: (bi, 0, 0)),
        ],
        out_shape=[
            jax.ShapeDtypeStruct((rows, width), BF16),
            jax.ShapeDtypeStruct((batch, chunk, width), F32),
        ],
        scratch_shapes=[pltpu.VMEM((groups, chunk, chunk), BF16)],
        compiler_params=pltpu.CompilerParams(
            dimension_semantics=("arbitrary", "arbitrary"),
            vmem_limit_bytes=_vmem_limit(blocks, 8 << 20)),
        name="cmlp_gate",
    )(z, ln_g, ln_b, w_s, b_s)


def _cmlp_gate_step_kernel(z_ref, lg_ref, lb_ref, scale_ref, bias_ref, o_ref, v_ref):
    width = o_ref.shape[1]
    vn = _layernorm(z_ref[:, width:], lg_ref[...], lb_ref[...])
    v_ref[...] = vn
    mix = scale_ref[...] * vn + bias_ref[...]
    o_ref[...] = (z_ref[:, :width] * mix).astype(BF16)


def _cmlp_gate_step(z, ln_g, ln_b, scale, bias, layer):
    rows, zw = z.shape
    width = zw // 2
    per_layer = pl.BlockSpec((None, 1, width), lambda i: (layer, 0, 0))
    return pl.pallas_call(
        _cmlp_gate_step_kernel,
        grid=(1,),
        in_specs=[pl.BlockSpec((rows, zw), lambda i: (0, 0)), per_layer, per_layer, per_layer, per_layer],
        out_specs=[pl.BlockSpec((rows, width), lambda i: (0, 0)), pl.BlockSpec((rows, width), lambda i: (0, 0))],
        out_shape=[jax.ShapeDtypeStruct((rows, width), BF16), jax.ShapeDtypeStruct((rows, width), F32)],
        compiler_params=pltpu.CompilerParams(dimension_semantics=("arbitrary",)),
        name="cmlp_gate_step",
    )(z, ln_g, ln_b, scale, bias)


def _set_tail_rows(stream, tail, start):
    pad = jnp.zeros((TAIL_ROWS - tail.shape[0], tail.shape[1]), stream.dtype)
    return lax.dynamic_update_slice(stream, jnp.concatenate([tail.astype(stream.dtype), pad], axis=0), (start, 0))


def kernel(x_prompt, x_sample, cache_k_win, cache_v_win, state_conv, w_in_even, w_out_even, conv_w, attn_sinks,
           w_in_cmlp, w_out_cmlp, ln_v_g, ln_v_b, w_spatial, b_spatial, w_ffn_up, w_ffn_down,
           g_mix_pre, g_mix_post, g_ffn_pre, g_ffn_post):
    batch, seq, d_model = x_prompt.shape
    dec_batch, dec_seq, _ = x_sample.shape
    assert dec_seq == 1, "the decode kernels take one new token per sequence"
    depth = g_mix_pre.shape[0]
    n_even, _, window, n_kv, head_dim = cache_k_win.shape
    assert head_dim == HEAD_DIM
    n_heads = attn_sinks.shape[1]
    q_w = n_heads * HEAD_DIM
    kv_w = n_kv * HEAD_DIM
    conv_dim = conv_w.shape[-1]
    n_odd, groups, chunk, _ = w_spatial.shape
    cmlp_w = w_out_cmlp.shape[1]
    n_prompt = batch * seq
    rows = n_prompt + TAIL_ROWS
    assert dec_batch <= TAIL_ROWS

    gains = [g.reshape(depth, 1, d_model) for g in (g_mix_pre, g_mix_post, g_ffn_pre, g_ffn_post)]
    g_mix_pre3, g_mix_post3, g_ffn_pre3, g_ffn_post3 = gains
    ln_g3 = ln_v_g.reshape(n_odd, 1, cmlp_w)
    ln_b3 = ln_v_b.reshape(n_odd, 1, cmlp_w)
    b_s4 = b_spatial.reshape(n_odd, groups, chunk, 1)
    step_scale = jnp.repeat(w_spatial[:, :, 0, 0], cmlp_w // groups, axis=1).reshape(n_odd, 1, cmlp_w)
    step_bias = jnp.repeat(b_spatial[:, :, 0], cmlp_w // groups, axis=1).reshape(n_odd, 1, cmlp_w)
    sinks3 = attn_sinks.reshape(n_even, n_heads, 1)
    cos_p, sin_p = _rope_tables(jnp.arange(seq))
    cos_s, sin_s = _rope_tables(PAST_LEN + jnp.arange(dec_seq))
    cos_s = jnp.tile(cos_s, (1, kv_w // LANES))
    sin_s = jnp.tile(sin_s, (1, kv_w // LANES))
    cache_k = cache_k_win.reshape(n_even, dec_batch, window, kv_w)
    cache_v = cache_v_win.reshape(n_even, dec_batch, window, kv_w)

    x, h = _rmsnorm_rows(x_prompt.reshape(n_prompt, d_model), g_mix_pre3, 0, out_rows=rows, tm=1024)
    xs = x_sample.reshape(dec_batch, d_model)
    _, hs = _rmsnorm_rows(xs, g_mix_pre3, 0, out_rows=dec_batch, tm=dec_batch)
    x = _set_tail_rows(x, xs, n_prompt)
    h = _set_tail_rows(h, hs, n_prompt)

    kp, vp, cp, up = [], [], [], []
    ks, vs, cs, us = [], [], [], []
    for i in range(depth):
        j = i // 2
        if i % 2 == 0:
            proj, w_out_b = _panel_matmul(h, w_in_even, j, w_out_even, j, act=_identity, tn=768, name="even_in")
            cat, k_p, v_p, z_p = _swa_conv(proj, cos_p, sin_p, attn_sinks[j], conv_w, j,
                                           batch=batch, blk=window, n_heads=n_heads, n_kv=n_kv)
            kp.append(k_p.reshape(batch, window, n_kv, HEAD_DIM))
            vp.append(v_p.reshape(batch, window, n_kv, HEAD_DIM))
            cp.append(z_p[:, SUBLANES - (CONV_WIDTH - 1):, :])

            proj_s = proj[n_prompt:n_prompt + dec_batch].astype(F32)
            q_rep = jnp.tile(proj_s[:, :q_w].reshape(dec_batch, n_heads, HEAD_DIM), (1, 1, n_kv))
            kv_new = proj_s[:, q_w:q_w + 2 * kv_w].reshape(dec_batch, 2, kv_w)
            gates = proj_s[:, q_w + 2 * kv_w:].reshape(dec_batch, 3, conv_dim)
            attn_s, conv_s, k_s, v_s, c_s = _swa_conv_step(
                q_rep, kv_new, gates, cache_k, cache_v, state_conv, cos_s, sin_s, sinks3, conv_w, j,
                nb=8, n_heads=n_heads, n_kv=n_kv)
            cat_s = jnp.concatenate([attn_s[:, :, :HEAD_DIM].reshape(dec_batch, q_w),
                                     conv_s.reshape(dec_batch, conv_dim)], axis=1)
            ks.append(k_s.reshape(dec_batch, window, n_kv, HEAD_DIM))
            vs.append(v_s.reshape(dec_batch, window, n_kv, HEAD_DIM))
            cs.append(c_s)
            mixed = _set_tail_rows(cat, cat_s, n_prompt)
            x, h = _resident_matmul(mixed, w_out_b, x, g_mix_post3, i, g_ffn_pre3, i, name="even_out")
        else:
            z, w_out_b = _panel_matmul(h, w_in_cmlp, j, w_out_cmlp, j, act=_gelu_exact, tn=1024, name="cmlp_in")
            gated, v_p = _cmlp_gate(z, ln_g3, ln_b3, w_spatial, b_s4, j, batch=batch, n_chunks=seq // chunk)
            up.append(v_p)
            gated_s, v_s = _cmlp_gate_step(z[n_prompt:n_prompt + dec_batch].astype(F32), ln_g3, ln_b3,
                                           step_scale, step_bias, j)
            us.append(v_s.reshape(dec_batch, dec_seq, cmlp_w))
            mixed = _set_tail_rows(gated, gated_s, n_prompt)
            x, h = _resident_matmul(mixed, w_out_b, x, g_mix_post3, i, g_ffn_pre3, i, name="cmlp_out")
        hidden, w_down_b = _panel_matmul(h, w_ffn_up, i, w_ffn_down, i, act=_relu_sq, tn=1024, name="ffn_up")
        last = i == depth - 1
        outs = _kstream_matmul(hidden, w_down_b, x, g_ffn_post3, i, g_mix_pre3, min(i + 1, depth - 1),
                               tk=1024, emit_next=not last, name="ffn_down")
        x, h = (outs[0], None) if last else outs

    return (x[:n_prompt].reshape(batch, seq, d_model), x[n_prompt:n_prompt + dec_batch].reshape(dec_batch, dec_seq, d_model),
            jnp.stack(kp), jnp.stack(vp), jnp.stack(cp), jnp.stack(up),
            jnp.stack(ks), jnp.stack(vs), jnp.stack(cs), jnp.stack(us))
```

```python
import functools
import math

import jax
import jax.numpy as jnp
from jax import lax
from jax.experimental import pallas as pl
from jax.experimental.pallas import tpu as pltpu

F32 = jnp.float32
BF16 = jnp.bfloat16

EPS = 1e-6
HEAD_DIM = 64
ROPE_THETA = 10000.0
PAST_LEN = 16384
CONV_WIDTH = 3

LANES = 128
SUBLANES = 8
BF16_ROWS = 16
MXU_COLS = 256
VMEM_BYTES_V7X = 64 * 1024 * 1024

TAIL_ROWS = 128
PANEL_ROW_TILES = 4
KSTREAM_ROW_TILES = 8
RESIDENT_ROW_TILES = 13
EPILOGUE_ROW_BLOCKS = 5
MIXER_BLOCKS = 5
LOG2_E = math.log2(math.e)


def _vmem_limit(block_bytes, temp_bytes):
    need = 2 * block_bytes + temp_bytes + (4 << 20)
    return int(min(need, VMEM_BYTES_V7X - (6 << 20)))


def _nbytes(shape, dtype):
    return math.prod(shape) * jnp.dtype(dtype).itemsize


def _rms_scale(x, g):
    ms = jnp.mean(x * x, axis=-1, keepdims=True)
    return x * lax.rsqrt(ms + EPS) * g


def _identity(y):
    return y


def _gelu_exact(y):
    return 0.5 * y * (1.0 + lax.erf(y * math.sqrt(0.5)))


def _relu_sq(y):
    r = jnp.maximum(y, 0.0)
    return r * r


def _stream_rmsnorm_kernel(xp_ref, xs_ref, g_ref, xo_ref, h_ref):
    last = pl.num_programs(0) - 1

    @pl.when(pl.program_id(0) < last)
    def _():
        x = xp_ref[...]
        xo_ref[...] = x
        h_ref[...] = _rms_scale(x, g_ref[...]).astype(BF16)

    @pl.when(pl.program_id(0) == last)
    def _():
        ns = xs_ref.shape[0]
        xs = xs_ref[...]
        xo_ref[...] = jnp.zeros_like(xo_ref)
        h_ref[...] = jnp.zeros_like(h_ref)
        xo_ref[:ns, :] = xs
        h_ref[:ns, :] = _rms_scale(xs, g_ref[...]).astype(BF16)


def _stream_rmsnorm(xp, xs, g, g_layer):
    n_prompt, d = xp.shape
    ns = xs.shape[0]
    steps = n_prompt // TAIL_ROWS + 1
    assert n_prompt % TAIL_ROWS == 0 and ns % BF16_ROWS == 0 and ns <= TAIL_ROWS
    rows = n_prompt + TAIL_ROWS
    block = pl.BlockSpec((TAIL_ROWS, d), lambda i: (i, 0))
    blocks = 2 * _nbytes((TAIL_ROWS, d), F32) + _nbytes((TAIL_ROWS, d), BF16)
    return pl.pallas_call(
        _stream_rmsnorm_kernel,
        grid=(steps,),
        in_specs=[pl.BlockSpec((TAIL_ROWS, d), lambda i: (jnp.minimum(i, steps - 2), 0)),
                  pl.BlockSpec((ns, d), lambda i: (0, 0)),
                  pl.BlockSpec((None, 1, d), lambda i: (g_layer, 0, 0))],
        out_specs=[block, block],
        out_shape=[jax.ShapeDtypeStruct((rows, d), F32), jax.ShapeDtypeStruct((rows, d), BF16)],
        compiler_params=pltpu.CompilerParams(
            dimension_semantics=("arbitrary",), vmem_limit_bytes=_vmem_limit(blocks, 4 << 20)),
        name="stream_rmsnorm",
    )(xp, xs, g)


def _panel_matmul_kernel(h_ref, w_ref, side_ref, o_ref, side_o_ref, wb_ref, *, act):
    def step(cast_panel):
        side_o_ref[...] = side_ref[...].astype(BF16)
        for c in range(0, o_ref.shape[1], MXU_COLS):
            cols = slice(c, c + MXU_COLS)
            if cast_panel:
                wb_ref[:, cols] = w_ref[:, cols].astype(BF16)
            y = jnp.dot(h_ref[...], wb_ref[:, cols], preferred_element_type=F32)
            o_ref[:, cols] = act(y).astype(o_ref.dtype)

    @pl.when(pl.program_id(1) == 0)
    def _():
        step(True)

    @pl.when(pl.program_id(1) > 0)
    def _():
        step(False)


def _panel_matmul(h, w, w_layer, side, side_layer, *, act, tn, name, panel_major=False):
    m, k = h.shape
    n = w.shape[-1]
    k2, n2 = side.shape[1:]
    tm = m // PANEL_ROW_TILES
    steps = (n // tn) * PANEL_ROW_TILES
    n_slabs = min(steps, k2 // LANES)
    slab = k2 // n_slabs
    assert m % (PANEL_ROW_TILES * BF16_ROWS) == 0 and n % tn == 0 and tn % MXU_COLS == 0
    assert k2 % n_slabs == 0 and slab % BF16_ROWS == 0
    blocks = (_nbytes((tm, k), BF16) + _nbytes((k, tn), F32) + _nbytes((tm, tn), BF16)
              + _nbytes((slab, n2), F32) + _nbytes((slab, n2), BF16))
    temps = _nbytes((k, tn), BF16) + 3 * _nbytes((tm, MXU_COLS), F32)
    slab_index = lambda j, i: jnp.minimum(j * PANEL_ROW_TILES + i, n_slabs - 1)
    if panel_major:
        out_spec = pl.BlockSpec((None, tm, tn), lambda j, i: (j, i, 0))
        out_shape = jax.ShapeDtypeStruct((n // tn, m, tn), BF16)
    else:
        out_spec = pl.BlockSpec((tm, tn), lambda j, i: (i, j))
        out_shape = jax.ShapeDtypeStruct((m, n), BF16)
    return pl.pallas_call(
        functools.partial(_panel_matmul_kernel, act=act),
        grid=(n // tn, PANEL_ROW_TILES),
        in_specs=[
            pl.BlockSpec((tm, k), lambda j, i: (i, 0)),
            pl.BlockSpec((None, k, tn), lambda j, i: (w_layer, 0, j)),
            pl.BlockSpec((None, slab, n2), lambda j, i: (side_layer, slab_index(j, i), 0)),
        ],
        out_specs=[out_spec, pl.BlockSpec((slab, n2), lambda j, i: (slab_index(j, i), 0))],
        out_shape=[out_shape, jax.ShapeDtypeStruct((k2, n2), BF16)],
        scratch_shapes=[pltpu.VMEM((k, tn), BF16)],
        compiler_params=pltpu.CompilerParams(
            dimension_semantics=("arbitrary", "arbitrary"),
            vmem_limit_bytes=_vmem_limit(blocks, temps)),
        name=name,
    )(h, w, side)


def _kstream_matmul_kernel(a_ref, w_ref, xk_ref, gp_ref, gn_ref, o_ref, *rest, nk, emit_next):
    if emit_next:
        hn_ref, x_sc = rest
    else:
        (x_sc,) = rest
    k = pl.program_id(1)
    tm, n = o_ref.shape
    rb = tm // EPILOGUE_ROW_BLOCKS
    for c in range(EPILOGUE_ROW_BLOCKS):
        @pl.when(k == c)
        def _(c=c):
            x_sc[c * rb:(c + 1) * rb, :] = xk_ref[...]

    def accumulate(rows, assign):
        a = a_ref[rows, :]
        for c in range(0, n, 2 * MXU_COLS):
            cols = slice(c, c + 2 * MXU_COLS)
            part = jnp.dot(a, w_ref[:, cols], preferred_element_type=F32)
            if assign:
                o_ref[rows, cols] = part
            else:
                o_ref[rows, cols] += part

    @pl.when(k == 0)
    def _():
        accumulate(slice(None), True)

    @pl.when((k > 0) & (k < nk - 1))
    def _():
        accumulate(slice(None), False)

    @pl.when(k == nk - 1)
    def _():
        for r in range(0, tm, rb):
            rows = slice(r, r + rb)
            accumulate(rows, False)
            y = x_sc[rows, :] + _rms_scale(o_ref[rows, :], gp_ref[...])
            o_ref[rows, :] = y
            if emit_next:
                hn_ref[rows, :] = _rms_scale(y, gn_ref[...]).astype(BF16)


def _kstream_matmul(a, w, x, g_post, g_post_layer, g_next, g_next_layer, *, emit_next, name):
    nk, m, tk = a.shape
    n = w.shape[-1]
    tm = m // KSTREAM_ROW_TILES
    rb = tm // EPILOGUE_ROW_BLOCKS
    assert m % KSTREAM_ROW_TILES == 0 and w.shape[0] == nk * tk and nk >= EPILOGUE_ROW_BLOCKS
    assert tm % (EPILOGUE_ROW_BLOCKS * BF16_ROWS) == 0 and n % (2 * MXU_COLS) == 0
    blocks = (_nbytes((tm, tk), BF16) + _nbytes((tk, n), BF16) + _nbytes((rb, n), F32)
              + _nbytes((tm, n), F32) + (_nbytes((tm, n), BF16) if emit_next else 0))
    temps = _nbytes((tm, n), F32) + 4 * _nbytes((tm // EPILOGUE_ROW_BLOCKS, n), F32)
    out_specs = [pl.BlockSpec((tm, n), lambda i, k: (i, 0))]
    out_shape = [jax.ShapeDtypeStruct((m, n), F32)]
    if emit_next:
        out_specs.append(pl.BlockSpec((tm, n), lambda i, k: (i, 0)))
        out_shape.append(jax.ShapeDtypeStruct((m, n), BF16))
    return pl.pallas_call(
        functools.partial(_kstream_matmul_kernel, nk=nk, emit_next=emit_next),
        grid=(KSTREAM_ROW_TILES, nk),
        in_specs=[
            pl.BlockSpec((None, tm, tk), lambda i, k: (k, i, 0)),
            pl.BlockSpec((tk, n), lambda i, k: (k, 0)),
            pl.BlockSpec((rb, n), lambda i, k: (i * EPILOGUE_ROW_BLOCKS + jnp.minimum(k, EPILOGUE_ROW_BLOCKS - 1), 0)),
            pl.BlockSpec((None, 1, n), lambda i, k: (g_post_layer, 0, 0)),
            pl.BlockSpec((None, 1, n), lambda i, k: (g_next_layer, 0, 0)),
        ],
        out_specs=out_specs,
        out_shape=out_shape,
        scratch_shapes=[pltpu.VMEM((tm, n), F32)],
        compiler_params=pltpu.CompilerParams(
            dimension_semantics=("arbitrary", "arbitrary"),
            vmem_limit_bytes=_vmem_limit(blocks, temps)),
        name=name,
    )(a, w, x, g_post, g_next)


def _resident_matmul_kernel(a_ref, w_ref, x_ref, gp_ref, gn_ref, o_ref, hn_ref):
    tm, n = o_ref.shape
    rb = tm // EPILOGUE_ROW_BLOCKS
    for r in range(0, tm, rb):
        rows = slice(r, r + rb)
        a = a_ref[rows, :]
        for c in range(0, n, 2 * MXU_COLS):
            cols = slice(c, c + 2 * MXU_COLS)
            o_ref[rows, cols] = jnp.dot(a, w_ref[:, cols], preferred_element_type=F32)
        y = x_ref[rows, :] + _rms_scale(o_ref[rows, :], gp_ref[...])
        o_ref[rows, :] = y
        hn_ref[rows, :] = _rms_scale(y, gn_ref[...]).astype(BF16)


def _resident_matmul(a, w, x, g_post, g_post_layer, g_next, g_next_layer, *, name):
    m, kdim = a.shape
    n = w.shape[-1]
    tm = m // RESIDENT_ROW_TILES
    assert m % RESIDENT_ROW_TILES == 0 and tm % (EPILOGUE_ROW_BLOCKS * BF16_ROWS) == 0 and n % (2 * MXU_COLS) == 0
    blocks = _nbytes((tm, kdim), BF16) + 2 * _nbytes((tm, n), F32) + _nbytes((tm, n), BF16)
    temps = _nbytes((kdim, n), BF16) + 4 * _nbytes((tm // EPILOGUE_ROW_BLOCKS, n), F32)
    return pl.pallas_call(
        _resident_matmul_kernel,
        grid=(RESIDENT_ROW_TILES,),
        in_specs=[
            pl.BlockSpec((tm, kdim), lambda i: (i, 0)),
            pl.BlockSpec((kdim, n), lambda i: (0, 0), pipeline_mode=pl.Buffered(1)),
            pl.BlockSpec((tm, n), lambda i: (i, 0)),
            pl.BlockSpec((None, 1, n), lambda i: (g_post_layer, 0, 0)),
            pl.BlockSpec((None, 1, n), lambda i: (g_next_layer, 0, 0)),
        ],
        out_specs=[pl.BlockSpec((tm, n), lambda i: (i, 0)), pl.BlockSpec((tm, n), lambda i: (i, 0))],
        out_shape=[jax.ShapeDtypeStruct((m, n), F32), jax.ShapeDtypeStruct((m, n), BF16)],
        compiler_params=pltpu.CompilerParams(
            dimension_semantics=("arbitrary",), vmem_limit_bytes=_vmem_limit(blocks, temps)),
        name=name,
    )(a, w, x, g_post, g_next)


def _rope_tables(positions):
    half = HEAD_DIM // 2
    inv_freq = ROPE_THETA ** (-jnp.arange(half, dtype=F32) / half)
    ang = positions.astype(F32)[:, None] * inv_freq[None, :]
    cos = jnp.cos(ang)
    sin = jnp.sin(ang)
    reps = LANES // HEAD_DIM
    cos_t = jnp.tile(jnp.concatenate([cos, cos], axis=1), (1, reps))
    sin_t = jnp.tile(jnp.concatenate([-sin, sin], axis=1), (1, reps))
    return cos_t, sin_t


def _rope(x, cos, sin):
    half = HEAD_DIM // 2
    w = x.shape[1]
    lane = lax.broadcasted_iota(jnp.int32, x.shape, 1)
    first_half = (lane % HEAD_DIM) < half
    rot = jnp.where(first_half, pltpu.roll(x, w - half, axis=1), pltpu.roll(x, half, axis=1))
    return x * cos + rot * sin


def _swa_conv_block(n, nb, p_ref, cos_q, sin_q, cos_k, sin_k, sink_ref, cw_ref,
                    cat_ref, kwin_ref, vwin_ref, ztail_ref,
                    kband_ref, vtband_ref, zprev_ref, *, n_heads, n_kv):
    blk = p_ref.shape[0]
    q_w = n_heads * HEAD_DIM
    kv_w = n_kv * HEAD_DIM
    conv_dim = cw_ref.shape[1]
    gqa = n_heads // n_kv

    @pl.when(n == 0)
    def _():
        kband_ref[...] = jnp.zeros_like(kband_ref)
        vtband_ref[...] = jnp.zeros_like(vtband_ref)
        zprev_ref[...] = jnp.zeros_like(zprev_ref)

    k_rot = _rope(p_ref[:, q_w:q_w + kv_w].astype(F32),
                  jnp.tile(cos_k, (1, kv_w // LANES)), jnp.tile(sin_k, (1, kv_w // LANES)))
    v_cur = p_ref[:, q_w + kv_w:q_w + 2 * kv_w].astype(F32)
    kband_ref[blk:, :] = k_rot.astype(BF16)
    vtband_ref[:, blk:] = v_cur.T.astype(BF16)

    @pl.when(n == nb - 1)
    def _():
        kwin_ref[...] = k_rot
        vwin_ref[...] = v_cur

    key = lax.broadcasted_iota(jnp.int32, (blk, blk), 0)
    qry = lax.broadcasted_iota(jnp.int32, (blk, blk), 1)
    own = key <= qry
    no_prev = jnp.where(own | (n > 0), 0.0, -jnp.inf)
    k_heads = [kband_ref[:, kv * HEAD_DIM:(kv + 1) * HEAD_DIM] for kv in range(n_kv)]

    heads_per_group = LANES // HEAD_DIM
    s_bands = []
    for pair in range(q_w // LANES):
        q_rot = _rope(p_ref[:, pair * LANES:(pair + 1) * LANES].astype(F32), cos_q, sin_q).astype(BF16)
        for sub in range(heads_per_group):
            kv = (pair * heads_per_group + sub) // gqa
            q_h = q_rot[:, sub * HEAD_DIM:(sub + 1) * HEAD_DIM]
            s_bands.append(lax.dot_general(k_heads[kv], q_h, (((1,), (1,)), ((), ())),
                                           preferred_element_type=F32))
    p_bands, inv_denoms = [], []
    for h, s_band in enumerate(s_bands):
        s = jnp.where(own, s_band[blk:], s_band[:blk]) + no_prev
        sink = sink_ref[h] * LOG2_E
        m = jnp.maximum(jnp.max(s, axis=0, keepdims=True), sink)
        p = jnp.exp2(s - m)
        denom = jnp.sum(p, axis=0, keepdims=True) + jnp.exp2(sink - m)
        p_bands.append(jnp.concatenate([jnp.where(own, 0.0, p), jnp.where(own, p, 0.0)], axis=0).astype(BF16))
        inv_denoms.append(1.0 / denom)
    for pair in range(q_w // LANES):
        outs_t = []
        for sub in range(heads_per_group):
            h = pair * heads_per_group + sub
            kv = h // gqa
            o_t = jnp.dot(vtband_ref[kv * HEAD_DIM:(kv + 1) * HEAD_DIM, :], p_bands[h],
                          preferred_element_type=F32)
            outs_t.append(o_t * inv_denoms[h])
        cat_ref[:, pair * LANES:(pair + 1) * LANES] = jnp.concatenate(outs_t, axis=0).T.astype(BF16)

    kband_ref[:blk, :] = kband_ref[blk:, :]
    vtband_ref[:, :blk] = vtband_ref[:, blk:]

    g_off = q_w + 2 * kv_w
    gate_b = p_ref[:, g_off:g_off + conv_dim].astype(F32)
    gate_c = p_ref[:, g_off + conv_dim:g_off + 2 * conv_dim].astype(F32)
    h_conv = p_ref[:, g_off + 2 * conv_dim:g_off + 3 * conv_dim].astype(F32)
    z = gate_c * h_conv
    z_prev = zprev_ref[...]
    top = lax.broadcasted_iota(jnp.int32, z_prev.shape, 0)

    def delayed(d):
        r = pltpu.roll(z, d, axis=0)
        head = jnp.where(top < d, pltpu.roll(z_prev, d, axis=0), r[:SUBLANES])
        return jnp.concatenate([head, r[SUBLANES:]], axis=0)

    conv = cw_ref[0:1, :] * delayed(2)
    conv = conv + cw_ref[1:2, :] * delayed(1)
    conv = conv + cw_ref[2:3, :] * z
    cat_ref[:, q_w:] = (gate_b * conv).astype(BF16)
    z_tail = z[blk - SUBLANES:, :]
    zprev_ref[...] = z_tail

    @pl.when(n == nb - 1)
    def _():
        ztail_ref[...] = z_tail


def _swa_conv_kernel(p_ref, cosq_ref, sinq_ref, cos_ref, sin_ref, sink_ref, cw_ref,
                     cat_ref, kwin_ref, vwin_ref, ztail_ref, *carries, n_heads, n_kv, blk):
    nb = cos_ref.shape[0] // blk
    for r in range(MIXER_BLOCKS):
        n = (pl.program_id(0) * MIXER_BLOCKS + r) % nb
        rows = pl.ds(r * blk, blk)
        pos = pl.ds(pl.multiple_of(n * blk, blk), blk)
        _swa_conv_block(n, nb, p_ref.at[rows], cosq_ref[pos, :], sinq_ref[pos, :], cos_ref[pos, :], sin_ref[pos, :],
                        sink_ref, cw_ref, cat_ref.at[rows], kwin_ref, vwin_ref, ztail_ref, *carries,
                        n_heads=n_heads, n_kv=n_kv)


def _per_sequence_index(nb, batch):
    assert MIXER_BLOCKS <= nb
    return lambda s: (jnp.minimum((s * MIXER_BLOCKS) // nb, batch - 1), 0, 0)


def _swa_conv(p, cos, sin, sinks, conv_w, layer, *, batch, blk, n_heads, n_kv):
    rows, pw = p.shape
    seq = cos.shape[0]
    nb = seq // blk
    step_rows = MIXER_BLOCKS * blk
    q_scale = HEAD_DIM ** -0.5 * LOG2_E
    table = pl.BlockSpec((seq, LANES), lambda s: (0, 0))
    q_w = n_heads * HEAD_DIM
    kv_w = n_kv * HEAD_DIM
    conv_dim = conv_w.shape[-1]
    assert rows % step_rows == 0
    blocks = _nbytes((step_rows, pw), BF16) + _nbytes((step_rows, q_w + conv_dim), BF16) + 4 * _nbytes((seq, LANES), F32)
    per_sequence = _per_sequence_index(nb, batch)
    return pl.pallas_call(
        functools.partial(_swa_conv_kernel, n_heads=n_heads, n_kv=n_kv, blk=blk),
        grid=(rows // step_rows,),
        in_specs=[
            pl.BlockSpec((step_rows, pw), lambda s: (s, 0)),
            table, table, table, table,
            pl.BlockSpec(memory_space=pltpu.SMEM),
            pl.BlockSpec((None, CONV_WIDTH, conv_dim), lambda s: (layer, 0, 0)),
        ],
        out_specs=[
            pl.BlockSpec((step_rows, q_w + conv_dim), lambda s: (s, 0)),
            pl.BlockSpec((None, blk, kv_w), per_sequence),
            pl.BlockSpec((None, blk, kv_w), per_sequence),
            pl.BlockSpec((None, SUBLANES, conv_dim), per_sequence),
        ],
        out_shape=[
            jax.ShapeDtypeStruct((rows, q_w + conv_dim), BF16),
            jax.ShapeDtypeStruct((batch, blk, kv_w), F32),
            jax.ShapeDtypeStruct((batch, blk, kv_w), F32),
            jax.ShapeDtypeStruct((batch, SUBLANES, conv_dim), F32),
        ],
        scratch_shapes=[
            pltpu.VMEM((2 * blk, kv_w), BF16),
            pltpu.VMEM((kv_w, 2 * blk), BF16),
            pltpu.VMEM((SUBLANES, conv_dim), F32),
        ],
        compiler_params=pltpu.CompilerParams(
            dimension_semantics=("arbitrary",), vmem_limit_bytes=_vmem_limit(blocks, 16 << 20)),
        name="swa_conv",
    )(p, cos * q_scale, sin * q_scale, cos, sin, sinks, conv_w)


def _swa_conv_step_kernel(q_ref, kv_ref, gates_ref, ck_ref, cv_ref, st_ref, cos_ref, sin_ref, sink_ref, cw_ref,
                          attn_ref, conv_ref, kout_ref, vout_ref, stout_ref, *, n_heads, n_kv):
    nb, window, kv_w = ck_ref.shape
    gqa = n_heads // n_kv
    row_h = lax.broadcasted_iota(jnp.int32, (n_heads, kv_w), 0)
    lane_h = lax.broadcasted_iota(jnp.int32, (n_heads, kv_w), 1)
    own = (lane_h // HEAD_DIM) == (row_h // gqa)
    key_pos = lax.broadcasted_iota(jnp.int32, (n_heads, window), 1)
    cos = cos_ref[...]
    sin = sin_ref[...]
    sink = sink_ref[...]
    scale = HEAD_DIM ** -0.5
    for i in range(nb):
        q = jnp.where(own, _rope(q_ref[i], cos, sin), 0.0)
        k_new = _rope(kv_ref[i, 0:1, :], cos, sin)
        v_new = kv_ref[i, 1:2, :]
        ck = ck_ref[i]
        cv = cv_ref[i]
        s = lax.dot_general(q.astype(BF16), ck.astype(BF16), (((1,), (1,)), ((), ())),
                            preferred_element_type=F32) * scale
        s = jnp.where(key_pos >= 1, s, -jnp.inf)
        s_new = jnp.sum(q * k_new, axis=-1, keepdims=True) * scale
        m = jnp.maximum(jnp.maximum(jnp.max(s, axis=-1, keepdims=True), s_new), sink)
        p = jnp.exp(s - m)
        p_new = jnp.exp(s_new - m)
        denom = jnp.sum(p, axis=-1, keepdims=True) + p_new + jnp.exp(sink - m)
        o = jnp.dot(p.astype(BF16), cv.astype(BF16), preferred_element_type=F32)
        o = o + p_new * v_new
        o = jnp.where(own, o / denom, 0.0)
        folded = o[:, :LANES]
        for c in range(LANES, kv_w, LANES):
            folded = folded + o[:, c:c + LANES]
        for shift in range(HEAD_DIM, LANES, HEAD_DIM):
            folded = folded + pltpu.roll(folded, shift, axis=1)
        attn_ref[i] = folded

        kout_ref[i] = jnp.where(lax.broadcasted_iota(jnp.int32, ck.shape, 0) == window - 1,
                                k_new, pltpu.roll(ck, window - 1, axis=0))
        vout_ref[i] = jnp.where(lax.broadcasted_iota(jnp.int32, cv.shape, 0) == window - 1,
                                v_new, pltpu.roll(cv, window - 1, axis=0))

        gate_b = gates_ref[i, 0:1, :]
        z = gates_ref[i, 1:2, :] * gates_ref[i, 2:3, :]
        conv = cw_ref[0:1, :] * st_ref[i, 0:1, :]
        conv = conv + cw_ref[1:2, :] * st_ref[i, 1:2, :]
        conv = conv + cw_ref[2:3, :] * z
        conv_ref[i] = gate_b * conv
        stout_ref[i, 0:1, :] = st_ref[i, 1:2, :]
        stout_ref[i, 1:2, :] = z


def _swa_conv_step(q_rep, kv_new, gates, cache_k, cache_v, state, cos, sin, sinks, conv_w, layer, *,
                   nb, n_heads, n_kv):
    db, window, kv_w = cache_k.shape[1:]
    conv_dim = conv_w.shape[-1]
    seq = lambda i: (i, 0, 0)
    lay = lambda i: (layer, i, 0, 0)
    return pl.pallas_call(
        functools.partial(_swa_conv_step_kernel, n_heads=n_heads, n_kv=n_kv),
        grid=(db // nb,),
        in_specs=[
            pl.BlockSpec((nb, n_heads, kv_w), seq),
            pl.BlockSpec((nb, 2, kv_w), seq),
            pl.BlockSpec((nb, 3, conv_dim), seq),
            pl.BlockSpec((None, nb, window, kv_w), lay),
            pl.BlockSpec((None, nb, window, kv_w), lay),
            pl.BlockSpec((None, nb, CONV_WIDTH - 1, conv_dim), lay),
            pl.BlockSpec((1, kv_w), lambda i: (0, 0)),
            pl.BlockSpec((1, kv_w), lambda i: (0, 0)),
            pl.BlockSpec((None, n_heads, 1), lambda i: (layer, 0, 0)),
            pl.BlockSpec((None, CONV_WIDTH, conv_dim), lambda i: (layer, 0, 0)),
        ],
        out_specs=[
            pl.BlockSpec((nb, n_heads, LANES), seq),
            pl.BlockSpec((nb, 1, conv_dim), seq),
            pl.BlockSpec((nb, window, kv_w), seq),
            pl.BlockSpec((nb, window, kv_w), seq),
            pl.BlockSpec((nb, CONV_WIDTH - 1, conv_dim), seq),
        ],
        out_shape=[
            jax.ShapeDtypeStruct((db, n_heads, LANES), F32),
            jax.ShapeDtypeStruct((db, 1, conv_dim), F32),
            jax.ShapeDtypeStruct((db, window, kv_w), F32),
            jax.ShapeDtypeStruct((db, window, kv_w), F32),
            jax.ShapeDtypeStruct((db, CONV_WIDTH - 1, conv_dim), F32),
        ],
        compiler_params=pltpu.CompilerParams(dimension_semantics=("arbitrary",)),
        name="swa_conv_step",
    )(q_rep, kv_new, gates, cache_k, cache_v, state, cos, sin, sinks, conv_w)


def _layernorm(v, g, b):
    vc = v - jnp.mean(v, axis=-1, keepdims=True)
    var = jnp.mean(vc * vc, axis=-1, keepdims=True)
    return vc * lax.rsqrt(var + EPS) * g + b


def _cmlp_gate_kernel(z_ref, lg_ref, lb_ref, ws_ref, bs_ref, o_ref, vlast_ref, wt_ref, *, n_chunks):
    groups, chunk = ws_ref.shape[:2]
    width = o_ref.shape[1]
    dg = width // groups

    @pl.when(pl.program_id(0) == 0)
    def _():
        row = lax.broadcasted_iota(jnp.int32, (chunk, chunk), 0)
        col = lax.broadcasted_iota(jnp.int32, (chunk, chunk), 1)
        for g in range(groups):
            wt_ref[g] = jnp.where(row >= col, ws_ref[g], 0.0).astype(BF16)

    for r in range(MIXER_BLOCKS):
        rows = slice(r * chunk, (r + 1) * chunk)
        vn = _layernorm(z_ref[rows, width:].astype(F32), lg_ref[...], lb_ref[...])

        @pl.when((pl.program_id(0) * MIXER_BLOCKS + r) % n_chunks == n_chunks - 1)
        def _(vn=vn):
            vlast_ref[...] = vn

        vn_b = vn.astype(BF16)
        for g in range(groups):
            lanes = slice(g * dg, (g + 1) * dg)
            mix = jnp.dot(wt_ref[g], vn_b[:, lanes], preferred_element_type=F32) + bs_ref[g]
            o_ref[rows, lanes] = (z_ref[rows, lanes].astype(F32) * mix).astype(BF16)


def _cmlp_gate(z, ln_g, ln_b, w_s, b_s, layer, *, batch, n_chunks):
    rows, zw = z.shape
    width = zw // 2
    groups, chunk = w_s.shape[1:3]
    step_rows = MIXER_BLOCKS * chunk
    assert rows % step_rows == 0
    blocks = _nbytes((step_rows, zw), BF16) + _nbytes((step_rows, width), BF16) + _nbytes((chunk, width), F32) \
        + 2 * _nbytes((groups, chunk, chunk), F32)
    return pl.pallas_call(
        functools.partial(_cmlp_gate_kernel, n_chunks=n_chunks),
        grid=(rows // step_rows,),
        in_specs=[
            pl.BlockSpec((step_rows, zw), lambda s: (s, 0)),
            pl.BlockSpec((None, 1, width), lambda s: (layer, 0, 0)),
            pl.BlockSpec((None, 1, width), lambda s: (layer, 0, 0)),
            pl.BlockSpec((None, groups, chunk, chunk), lambda s: (layer, 0, 0, 0)),
            pl.BlockSpec((None, groups, chunk, 1), lambda s: (layer, 0, 0, 0)),
        ],
        out_specs=[
            pl.BlockSpec((step_rows, width), lambda s: (s, 0)),
            pl.BlockSpec((None, chunk, width), _per_sequence_index(n_chunks, batch)),
        ],
        out_shape=[
            jax.ShapeDtypeStruct((rows, width), BF16),
            jax.ShapeDtypeStruct((batch, chunk, width), F32),
        ],
        scratch_shapes=[pltpu.VMEM((groups, chunk, chunk), BF16)],
        compiler_params=pltpu.CompilerParams(
            dimension_semantics=("arbitrary",), vmem_limit_bytes=_vmem_limit(blocks, 12 << 20)),
        name="cmlp_gate",
    )(z, ln_g, ln_b, w_s, b_s)


def _cmlp_gate_step_kernel(z_ref, lg_ref, lb_ref, scale_ref, bias_ref, o_ref, v_ref):
    width = o_ref.shape[1]
    vn = _layernorm(z_ref[:, width:], lg_ref[...], lb_ref[...])
    v_ref[...] = vn
    mix = scale_ref[...] * vn + bias_ref[...]
    o_ref[...] = (z_ref[:, :width] * mix).astype(BF16)


def _cmlp_gate_step(z, ln_g, ln_b, scale, bias, layer):
    rows, zw = z.shape
    width = zw // 2
    per_layer = pl.BlockSpec((None, 1, width), lambda i: (layer, 0, 0))
    return pl.pallas_call(
        _cmlp_gate_step_kernel,
        grid=(1,),
        in_specs=[pl.BlockSpec((rows, zw), lambda i: (0, 0)), per_layer, per_layer, per_layer, per_layer],
        out_specs=[pl.BlockSpec((rows, width), lambda i: (0, 0)), pl.BlockSpec((rows, width), lambda i: (0, 0))],
        out_shape=[jax.ShapeDtypeStruct((rows, width), BF16), jax.ShapeDtypeStruct((rows, width), F32)],
        compiler_params=pltpu.CompilerParams(dimension_semantics=("arbitrary",)),
        name="cmlp_gate_step",
    )(z, ln_g, ln_b, scale, bias)


def _set_tail_rows(stream, tail, start):
    pad = jnp.zeros((TAIL_ROWS - tail.shape[0], tail.shape[1]), stream.dtype)
    return lax.dynamic_update_slice(stream, jnp.concatenate([tail.astype(stream.dtype), pad], axis=0), (start, 0))


def kernel(x_prompt, x_sample, cache_k_win, cache_v_win, state_conv, w_in_even, w_out_even, conv_w, attn_sinks,
           w_in_cmlp, w_out_cmlp, ln_v_g, ln_v_b, w_spatial, b_spatial, w_ffn_up, w_ffn_down,
           g_mix_pre, g_mix_post, g_ffn_pre, g_ffn_post):
    batch, seq, d_model = x_prompt.shape
    dec_batch, dec_seq, _ = x_sample.shape
    assert dec_seq == 1, "the decode kernels take one new token per sequence"
    depth = g_mix_pre.shape[0]
    n_even, _, window, n_kv, head_dim = cache_k_win.shape
    assert head_dim == HEAD_DIM
    n_heads = attn_sinks.shape[1]
    q_w = n_heads * HEAD_DIM
    kv_w = n_kv * HEAD_DIM
    conv_dim = conv_w.shape[-1]
    n_odd, groups, chunk, _ = w_spatial.shape
    cmlp_w = w_out_cmlp.shape[1]
    n_prompt = batch * seq
    rows = n_prompt + TAIL_ROWS
    assert dec_batch <= TAIL_ROWS

    gains = [g.reshape(depth, 1, d_model) for g in (g_mix_pre, g_mix_post, g_ffn_pre, g_ffn_post)]
    g_mix_pre3, g_mix_post3, g_ffn_pre3, g_ffn_post3 = gains
    ln_g3 = ln_v_g.reshape(n_odd, 1, cmlp_w)
    ln_b3 = ln_v_b.reshape(n_odd, 1, cmlp_w)
    b_s4 = b_spatial.reshape(n_odd, groups, chunk, 1)
    step_scale = jnp.repeat(w_spatial[:, :, 0, 0], cmlp_w // groups, axis=1).reshape(n_odd, 1, cmlp_w)
    step_bias = jnp.repeat(b_spatial[:, :, 0], cmlp_w // groups, axis=1).reshape(n_odd, 1, cmlp_w)
    sinks3 = attn_sinks.reshape(n_even, n_heads, 1)
    cos_p, sin_p = _rope_tables(jnp.arange(seq))
    cos_s, sin_s = _rope_tables(PAST_LEN + jnp.arange(dec_seq))
    cos_s = jnp.tile(cos_s, (1, kv_w // LANES))
    sin_s = jnp.tile(sin_s, (1, kv_w // LANES))
    cache_k = cache_k_win.reshape(n_even, dec_batch, window, kv_w)
    cache_v = cache_v_win.reshape(n_even, dec_batch, window, kv_w)

    x, h = _stream_rmsnorm(x_prompt.reshape(n_prompt, d_model), x_sample.reshape(dec_batch, d_model), g_mix_pre3, 0)

    kp, vp, cp, up = [], [], [], []
    ks, vs, cs, us = [], [], [], []
    for i in range(depth):
        j = i // 2
        if i % 2 == 0:
            proj, w_out_b = _panel_matmul(h, w_in_even, j, w_out_even, j, act=_identity, tn=768, name="even_in")
            cat, k_p, v_p, z_p = _swa_conv(proj, cos_p, sin_p, attn_sinks[j], conv_w, j,
                                           batch=batch, blk=window, n_heads=n_heads, n_kv=n_kv)
            kp.append(k_p.reshape(batch, window, n_kv, HEAD_DIM))
            vp.append(v_p.reshape(batch, window, n_kv, HEAD_DIM))
            cp.append(z_p[:, SUBLANES - (CONV_WIDTH - 1):, :])

            proj_s = proj[n_prompt:n_prompt + dec_batch].astype(F32)
            q_rep = jnp.tile(proj_s[:, :q_w].reshape(dec_batch, n_heads, HEAD_DIM), (1, 1, n_kv))
            kv_new = proj_s[:, q_w:q_w + 2 * kv_w].reshape(dec_batch, 2, kv_w)
            gates = proj_s[:, q_w + 2 * kv_w:].reshape(dec_batch, 3, conv_dim)
            attn_s, conv_s, k_s, v_s, c_s = _swa_conv_step(
                q_rep, kv_new, gates, cache_k, cache_v, state_conv, cos_s, sin_s, sinks3, conv_w, j,
                nb=8, n_heads=n_heads, n_kv=n_kv)
            cat_s = jnp.concatenate([attn_s[:, :, :HEAD_DIM].reshape(dec_batch, q_w),
                                     conv_s.reshape(dec_batch, conv_dim)], axis=1)
            ks.append(k_s.reshape(dec_batch, window, n_kv, HEAD_DIM))
            vs.append(v_s.reshape(dec_batch, window, n_kv, HEAD_DIM))
            cs.append(c_s)
            mixed = _set_tail_rows(cat, cat_s, n_prompt)
            x, h = _resident_matmul(mixed, w_out_b, x, g_mix_post3, i, g_ffn_pre3, i, name="even_out")
        else:
            z, w_out_b = _panel_matmul(h, w_in_cmlp, j, w_out_cmlp, j, act=_gelu_exact, tn=1024, name="cmlp_in")
            gated, v_p = _cmlp_gate(z, ln_g3, ln_b3, w_spatial, b_s4, j, batch=batch, n_chunks=seq // chunk)
            up.append(v_p)
            gated_s, v_s = _cmlp_gate_step(z[n_prompt:n_prompt + dec_batch].astype(F32), ln_g3, ln_b3,
                                           step_scale, step_bias, j)
            us.append(v_s.reshape(dec_batch, dec_seq, cmlp_w))
            mixed = _set_tail_rows(gated, gated_s, n_prompt)
            x, h = _resident_matmul(mixed, w_out_b, x, g_mix_post3, i, g_ffn_pre3, i, name="cmlp_out")
        hidden, w_down_b = _panel_matmul(h, w_ffn_up, i, w_ffn_down, i, act=_relu_sq, tn=1024, name="ffn_up",
                                         panel_major=True)
        last = i == depth - 1
        outs = _kstream_matmul(hidden, w_down_b, x, g_ffn_post3, i, g_mix_pre3, min(i + 1, depth - 1),
                               emit_next=not last, name="ffn_down")
        x, h = (outs[0], None) if last else outs

    return (x[:n_prompt].reshape(batch, seq, d_model), x[n_prompt:n_prompt + dec_batch].reshape(dec_batch, dec_seq, d_model),
            jnp.stack(kp), jnp.stack(vp), jnp.stack(cp), jnp.stack(up),
            jnp.stack(ks), jnp.stack(vs), jnp.stack(cs), jnp.stack(us))
```

```python
import functools
import math

import jax
import jax.numpy as jnp
from jax import lax
from jax.experimental import pallas as pl
from jax.experimental.pallas import tpu as pltpu

F32 = jnp.float32
BF16 = jnp.bfloat16

EPS = 1e-6
HEAD_DIM = 64
ROPE_THETA = 10000.0
PAST_LEN = 16384
CONV_WIDTH = 3

LANES = 128
SUBLANES = 8
BF16_ROWS = 16
MXU_COLS = 256
VMEM_BYTES_V7X = 64 * 1024 * 1024

TAIL_ROWS = 128
PANEL_ROW_TILES = 4
KSTREAM_ROW_TILES = 8
EPILOGUE_ROW_BLOCKS = 5
MIXER_BLOCKS = 5
LOG2_E = math.log2(math.e)


def _vmem_limit(block_bytes, temp_bytes):
    need = 2 * block_bytes + temp_bytes + (4 << 20)
    return int(min(need, VMEM_BYTES_V7X - (6 << 20)))


def _nbytes(shape, dtype):
    return math.prod(shape) * jnp.dtype(dtype).itemsize


def _rms_scale(x, g):
    ms = jnp.mean(x * x, axis=-1, keepdims=True)
    return x * lax.rsqrt(ms + EPS) * g


def _identity(y):
    return y


def _gelu_exact(y):
    return 0.5 * y * (1.0 + lax.erf(y * math.sqrt(0.5)))


def _relu_sq(y):
    r = jnp.maximum(y, 0.0)
    return r * r


def _stream_rmsnorm_kernel(*refs):
    xp_refs = refs[:MIXER_BLOCKS]
    xs_ref, g_ref, xo_ref, h_ref = refs[MIXER_BLOCKS:]
    tail_step = pl.program_id(0) == pl.num_programs(0) - 1
    for r, xp_ref in enumerate(xp_refs):
        x = xp_ref[...]
        if r == MIXER_BLOCKS - 1:
            xs = xs_ref[...]
            tail = jnp.concatenate([xs, jnp.zeros((TAIL_ROWS - xs.shape[0], xs.shape[1]), xs.dtype)], axis=0)
            x = jnp.where(tail_step, tail, x)
        rows = slice(r * TAIL_ROWS, (r + 1) * TAIL_ROWS)
        xo_ref[rows, :] = x
        h_ref[rows, :] = _rms_scale(x, g_ref[...]).astype(BF16)


def _stream_rmsnorm(xp, xs, g, g_layer):
    n_prompt, d = xp.shape
    ns = xs.shape[0]
    n_blocks = n_prompt // TAIL_ROWS
    rows = n_prompt + TAIL_ROWS
    tm = MIXER_BLOCKS * TAIL_ROWS
    assert n_prompt % TAIL_ROWS == 0 and rows % tm == 0 and ns % SUBLANES == 0 and ns <= TAIL_ROWS
    prompt_block = lambda r: pl.BlockSpec(
        (None, TAIL_ROWS, d), lambda s: (jnp.minimum(s * MIXER_BLOCKS + r, n_blocks - 1), 0, 0))
    row_tile = pl.BlockSpec((tm, d), lambda s: (s, 0))
    blocks = 2 * _nbytes((tm, d), F32) + _nbytes((tm, d), BF16)
    return pl.pallas_call(
        _stream_rmsnorm_kernel,
        grid=(rows // tm,),
        in_specs=[prompt_block(r) for r in range(MIXER_BLOCKS)] + [
            pl.BlockSpec((ns, d), lambda s: (0, 0)),
            pl.BlockSpec((None, 1, d), lambda s: (g_layer, 0, 0))],
        out_specs=[row_tile, row_tile],
        out_shape=[jax.ShapeDtypeStruct((rows, d), F32), jax.ShapeDtypeStruct((rows, d), BF16)],
        compiler_params=pltpu.CompilerParams(
            dimension_semantics=("arbitrary",), vmem_limit_bytes=_vmem_limit(blocks, 4 << 20)),
        name="stream_rmsnorm",
    )(*([xp.reshape(n_blocks, TAIL_ROWS, d)] * MIXER_BLOCKS), xs, g)


def _panel_matmul_kernel(h_ref, w_ref, side_ref, o_ref, side_o_ref, wb_ref, *, act):
    def step(cast_panel):
        side_o_ref[...] = side_ref[...].astype(BF16)
        for c in range(0, o_ref.shape[1], MXU_COLS):
            cols = slice(c, c + MXU_COLS)
            if cast_panel:
                wb_ref[:, cols] = w_ref[:, cols].astype(BF16)
            y = jnp.dot(h_ref[...], wb_ref[:, cols], preferred_element_type=F32)
            o_ref[:, cols] = act(y).astype(o_ref.dtype)

    @pl.when(pl.program_id(1) == 0)
    def _():
        step(True)

    @pl.when(pl.program_id(1) > 0)
    def _():
        step(False)


def _panel_matmul(h, w, w_layer, side, side_layer, *, act, tn, name, panel_major=False):
    m, k = h.shape
    n = w.shape[-1]
    k2, n2 = side.shape[1:]
    tm = m // PANEL_ROW_TILES
    steps = (n // tn) * PANEL_ROW_TILES
    n_slabs = min(steps, k2 // LANES)
    slab = k2 // n_slabs
    assert m % (PANEL_ROW_TILES * BF16_ROWS) == 0 and n % tn == 0 and tn % MXU_COLS == 0
    assert k2 % n_slabs == 0 and slab % BF16_ROWS == 0
    blocks = (_nbytes((tm, k), BF16) + _nbytes((k, tn), F32) + _nbytes((tm, tn), BF16)
              + _nbytes((slab, n2), F32) + _nbytes((slab, n2), BF16))
    temps = _nbytes((k, tn), BF16) + 3 * _nbytes((tm, MXU_COLS), F32)
    slab_index = lambda j, i: jnp.minimum(j * PANEL_ROW_TILES + i, n_slabs - 1)
    if panel_major:
        out_spec = pl.BlockSpec((None, tm, tn), lambda j, i: (j, i, 0))
        out_shape = jax.ShapeDtypeStruct((n // tn, m, tn), BF16)
    else:
        out_spec = pl.BlockSpec((tm, tn), lambda j, i: (i, j))
        out_shape = jax.ShapeDtypeStruct((m, n), BF16)
    return pl.pallas_call(
        functools.partial(_panel_matmul_kernel, act=act),
        grid=(n // tn, PANEL_ROW_TILES),
        in_specs=[
            pl.BlockSpec((tm, k), lambda j, i: (i, 0)),
            pl.BlockSpec((None, k, tn), lambda j, i: (w_layer, 0, j)),
            pl.BlockSpec((None, slab, n2), lambda j, i: (side_layer, slab_index(j, i), 0)),
        ],
        out_specs=[out_spec, pl.BlockSpec((slab, n2), lambda j, i: (slab_index(j, i), 0))],
        out_shape=[out_shape, jax.ShapeDtypeStruct((k2, n2), BF16)],
        scratch_shapes=[pltpu.VMEM((k, tn), BF16)],
        compiler_params=pltpu.CompilerParams(
            dimension_semantics=("arbitrary", "arbitrary"),
            vmem_limit_bytes=_vmem_limit(blocks, temps)),
        name=name,
    )(h, w, side)


def _kstream_matmul_kernel(a_ref, w_ref, xk_ref, gp_ref, gn_ref, o_ref, *rest, nk, emit_next):
    if emit_next:
        hn_ref, x_sc = rest
    else:
        (x_sc,) = rest
    k = pl.program_id(1)
    tm, n = o_ref.shape
    rb = tm // EPILOGUE_ROW_BLOCKS
    for c in range(EPILOGUE_ROW_BLOCKS):
        @pl.when(k == c)
        def _(c=c):
            x_sc[c * rb:(c + 1) * rb, :] = xk_ref[...]

    def accumulate(rows, assign):
        a = a_ref[rows, :]
        for c in range(0, n, 2 * MXU_COLS):
            cols = slice(c, c + 2 * MXU_COLS)
            part = jnp.dot(a, w_ref[:, cols], preferred_element_type=F32)
            if assign:
                o_ref[rows, cols] = part
            else:
                o_ref[rows, cols] += part

    @pl.when(k == 0)
    def _():
        accumulate(slice(None), True)

    @pl.when((k > 0) & (k < nk - 1))
    def _():
        accumulate(slice(None), False)

    @pl.when(k == nk - 1)
    def _():
        for r in range(0, tm, rb):
            rows = slice(r, r + rb)
            accumulate(rows, False)
            y = x_sc[rows, :] + _rms_scale(o_ref[rows, :], gp_ref[...])
            o_ref[rows, :] = y
            if emit_next:
                hn_ref[rows, :] = _rms_scale(y, gn_ref[...]).astype(BF16)


def _kstream_matmul(a, w, x, g_post, g_post_layer, g_next, g_next_layer, *, emit_next, name):
    nk, m, tk = a.shape
    n = w.shape[-1]
    tm = m // KSTREAM_ROW_TILES
    rb = tm // EPILOGUE_ROW_BLOCKS
    assert m % KSTREAM_ROW_TILES == 0 and w.shape[0] == nk * tk and nk >= EPILOGUE_ROW_BLOCKS
    assert tm % (EPILOGUE_ROW_BLOCKS * BF16_ROWS) == 0 and n % (2 * MXU_COLS) == 0
    blocks = (_nbytes((tm, tk), BF16) + _nbytes((tk, n), BF16) + _nbytes((rb, n), F32)
              + _nbytes((tm, n), F32) + (_nbytes((tm, n), BF16) if emit_next else 0))
    temps = _nbytes((tm, n), F32) + 4 * _nbytes((tm // EPILOGUE_ROW_BLOCKS, n), F32)
    out_specs = [pl.BlockSpec((tm, n), lambda i, k: (i, 0))]
    out_shape = [jax.ShapeDtypeStruct((m, n), F32)]
    if emit_next:
        out_specs.append(pl.BlockSpec((tm, n), lambda i, k: (i, 0)))
        out_shape.append(jax.ShapeDtypeStruct((m, n), BF16))
    return pl.pallas_call(
        functools.partial(_kstream_matmul_kernel, nk=nk, emit_next=emit_next),
        grid=(KSTREAM_ROW_TILES, nk),
        in_specs=[
            pl.BlockSpec((None, tm, tk), lambda i, k: (k, i, 0)),
            pl.BlockSpec((tk, n), lambda i, k: (k, 0)),
            pl.BlockSpec((rb, n), lambda i, k: (i * EPILOGUE_ROW_BLOCKS + jnp.minimum(k, EPILOGUE_ROW_BLOCKS - 1), 0)),
            pl.BlockSpec((None, 1, n), lambda i, k: (g_post_layer, 0, 0)),
            pl.BlockSpec((None, 1, n), lambda i, k: (g_next_layer, 0, 0)),
        ],
        out_specs=out_specs,
        out_shape=out_shape,
        scratch_shapes=[pltpu.VMEM((tm, n), F32)],
        compiler_params=pltpu.CompilerParams(
            dimension_semantics=("arbitrary", "arbitrary"),
            vmem_limit_bytes=_vmem_limit(blocks, temps)),
        name=name,
    )(a, w, x, g_post, g_next)


def _project_rows(a_sc, rows, w_ref, o_ref):
    for c in range(0, o_ref.shape[1], 2 * MXU_COLS):
        cols = slice(c, c + 2 * MXU_COLS)
        o_ref[rows, cols] = jnp.dot(a_sc[rows, :], w_ref[:, cols], preferred_element_type=F32)


def _finish_rows(rows, x_ref, gp_ref, gn_ref, o_ref, hn_ref):
    y = x_ref[rows, :] + _rms_scale(o_ref[rows, :], gp_ref[...])
    o_ref[rows, :] = y
    hn_ref[rows, :] = _rms_scale(y, gn_ref[...]).astype(BF16)


def _mix_project_call(kernel_fn, mix_inputs, mix_specs, tail, w, x, g_post, g_post_layer, g_next, g_next_layer, *,
                      blk, scratch_shapes, mix_bytes, name):
    m, n = x.shape
    kdim = w.shape[0]
    tm = MIXER_BLOCKS * blk
    assert m % tm == 0 and n % (2 * MXU_COLS) == 0 and tail.shape == (blk, kdim)
    blocks = mix_bytes + _nbytes((blk, kdim), BF16) + 2 * _nbytes((tm, n), F32) + _nbytes((tm, n), BF16)
    temps = _nbytes((kdim, n), BF16) + _nbytes((tm, kdim), BF16) + 4 * _nbytes((blk, n), F32) + (4 << 20)
    row_tile = lambda cols: pl.BlockSpec((tm, cols), lambda s: (s, 0))
    return pl.pallas_call(
        kernel_fn,
        grid=(m // tm,),
        in_specs=list(mix_specs) + [
            pl.BlockSpec((blk, kdim), lambda s: (0, 0)),
            pl.BlockSpec((kdim, n), lambda s: (0, 0), pipeline_mode=pl.Buffered(1)),
            row_tile(n),
            pl.BlockSpec((None, 1, n), lambda s: (g_post_layer, 0, 0)),
            pl.BlockSpec((None, 1, n), lambda s: (g_next_layer, 0, 0)),
        ],
        out_specs=[row_tile(n), row_tile(n)],
        out_shape=[jax.ShapeDtypeStruct((m, n), F32), jax.ShapeDtypeStruct((m, n), BF16)],
        scratch_shapes=[pltpu.VMEM((tm, kdim), BF16)] + list(scratch_shapes),
        compiler_params=pltpu.CompilerParams(
            dimension_semantics=("arbitrary",), vmem_limit_bytes=_vmem_limit(blocks, temps)),
        name=name,
    )(*mix_inputs, tail, w, x, g_post, g_next)


def _rope_tables(positions):
    half = HEAD_DIM // 2
    inv_freq = ROPE_THETA ** (-jnp.arange(half, dtype=F32) / half)
    ang = positions.astype(F32)[:, None] * inv_freq[None, :]
    cos = jnp.cos(ang)
    sin = jnp.sin(ang)
    reps = LANES // HEAD_DIM
    cos_t = jnp.tile(jnp.concatenate([cos, cos], axis=1), (1, reps))
    sin_t = jnp.tile(jnp.concatenate([-sin, sin], axis=1), (1, reps))
    return cos_t, sin_t


def _rope(x, cos, sin):
    half = HEAD_DIM // 2
    w = x.shape[1]
    lane = lax.broadcasted_iota(jnp.int32, x.shape, 1)
    first_half = (lane % HEAD_DIM) < half
    rot = jnp.where(first_half, pltpu.roll(x, w - half, axis=1), pltpu.roll(x, half, axis=1))
    return x * cos + rot * sin


def _rotated_keys(p_ref, cos, sin, *, n_heads, n_kv):
    q_w = n_heads * HEAD_DIM
    kv_w = n_kv * HEAD_DIM
    return _rope(p_ref[:, q_w:q_w + kv_w].astype(F32),
                 jnp.tile(cos, (1, kv_w // LANES)), jnp.tile(sin, (1, kv_w // LANES)))


def _swa_scores(r, p_ref, cos, sin, kslots_ref, vtslots_ref, *, n_heads, n_kv):
    blk = p_ref.shape[0]
    q_w = n_heads * HEAD_DIM
    kv_w = n_kv * HEAD_DIM
    gqa = n_heads // n_kv
    own_slot = slice((r + 1) * blk, (r + 2) * blk)
    band = slice(r * blk, (r + 2) * blk)
    kslots_ref[own_slot, :] = _rotated_keys(p_ref, cos, sin, n_heads=n_heads, n_kv=n_kv).astype(BF16)
    vtslots_ref[:, own_slot] = p_ref[:, q_w + kv_w:q_w + 2 * kv_w].astype(F32).T.astype(BF16)
    q_scale = HEAD_DIM ** -0.5 * LOG2_E
    cos_q = cos * q_scale
    sin_q = sin * q_scale
    k_heads = [kslots_ref[band, kv * HEAD_DIM:(kv + 1) * HEAD_DIM] for kv in range(n_kv)]
    heads_per_group = LANES // HEAD_DIM
    s_bands = []
    for pair in range(q_w // LANES):
        q_rot = _rope(p_ref[:, pair * LANES:(pair + 1) * LANES].astype(F32), cos_q, sin_q).astype(BF16)
        for sub in range(heads_per_group):
            kv = (pair * heads_per_group + sub) // gqa
            q_h = q_rot[:, sub * HEAD_DIM:(sub + 1) * HEAD_DIM]
            s_bands.append(lax.dot_general(k_heads[kv], q_h, (((1,), (1,)), ((), ())),
                                           preferred_element_type=F32))
    return s_bands


def _conv_gates(p_ref, *, n_heads, n_kv, conv_dim):
    g_off = (n_heads + 2 * n_kv) * HEAD_DIM
    return [p_ref[:, g_off + i * conv_dim:g_off + (i + 1) * conv_dim].astype(F32) for i in range(CONV_WIDTH)]


def _swa_mix(r, n, s_bands, z_prev, p_ref, sink_ref, cw_ref, out_ref, vtslots_ref, *, n_heads, n_kv):
    blk = p_ref.shape[0]
    q_w = n_heads * HEAD_DIM
    conv_dim = cw_ref.shape[1]
    gqa = n_heads // n_kv
    heads_per_group = LANES // HEAD_DIM
    rows = slice(r * blk, (r + 1) * blk)
    band = slice(r * blk, (r + 2) * blk)
    key = lax.broadcasted_iota(jnp.int32, (blk, blk), 0)
    qry = lax.broadcasted_iota(jnp.int32, (blk, blk), 1)
    own = key <= qry
    no_prev = jnp.where(own | (n > 0), 0.0, -jnp.inf)
    p_bands, inv_denoms = [], []
    for h, s_band in enumerate(s_bands):
        s = jnp.where(own, s_band[blk:], s_band[:blk]) + no_prev
        sink = sink_ref[h] * LOG2_E
        m = jnp.maximum(jnp.max(s, axis=0, keepdims=True), sink)
        p = jnp.exp2(s - m)
        denom = jnp.sum(p, axis=0, keepdims=True) + jnp.exp2(sink - m)
        p_bands.append(jnp.concatenate([jnp.where(own, 0.0, p), jnp.where(own, p, 0.0)], axis=0).astype(BF16))
        inv_denoms.append(1.0 / denom)
    for pair in range(q_w // LANES):
        outs_t = []
        for sub in range(heads_per_group):
            h = pair * heads_per_group + sub
            kv = h // gqa
            o_t = jnp.dot(vtslots_ref[kv * HEAD_DIM:(kv + 1) * HEAD_DIM, band], p_bands[h],
                          preferred_element_type=F32)
            outs_t.append(o_t * inv_denoms[h])
        out_ref[rows, pair * LANES:(pair + 1) * LANES] = jnp.concatenate(outs_t, axis=0).T.astype(BF16)

    gate_b, gate_c, h_conv = _conv_gates(p_ref, n_heads=n_heads, n_kv=n_kv, conv_dim=conv_dim)
    z = gate_c * h_conv
    z_prev = jnp.where(n > 0, z_prev, 0.0)
    top = lax.broadcasted_iota(jnp.int32, z_prev.shape, 0)

    def delayed(d):
        rolled = pltpu.roll(z, d, axis=0)
        head = jnp.where(top < d, pltpu.roll(z_prev, d, axis=0), rolled[:SUBLANES])
        return jnp.concatenate([head, rolled[SUBLANES:]], axis=0)

    conv = cw_ref[0:1, :] * delayed(2)
    conv = conv + cw_ref[1:2, :] * delayed(1)
    conv = conv + cw_ref[2:3, :] * z
    out_ref[rows, q_w:] = (gate_b * conv).astype(BF16)
    return z[blk - SUBLANES:, :]


def _swa_out_kernel(p_ref, cos_ref, sin_ref, sink_ref, cw_ref, tail_ref, w_ref, x_ref, gp_ref, gn_ref,
                    o_ref, hn_ref, a_sc, kslots_ref, vtslots_ref, zprev_ref, znext_ref, *, n_heads, n_kv, blk):
    nb = cos_ref.shape[0] // blk
    step = pl.program_id(0)
    tail_step = step == pl.num_programs(0) - 1
    heads = dict(n_heads=n_heads, n_kv=n_kv)
    first_slot = slice(0, blk)
    last_slot = slice(MIXER_BLOCKS * blk, (MIXER_BLOCKS + 1) * blk)

    @pl.when(step == 0)
    def _():
        kslots_ref[first_slot, :] = jnp.zeros((blk, kslots_ref.shape[1]), BF16)
        vtslots_ref[:, first_slot] = jnp.zeros((vtslots_ref.shape[0], blk), BF16)
        zprev_ref[...] = jnp.zeros_like(zprev_ref)

    @pl.when(step > 0)
    def _():
        kslots_ref[first_slot, :] = kslots_ref[last_slot, :]
        vtslots_ref[:, first_slot] = vtslots_ref[:, last_slot]
        zprev_ref[...] = znext_ref[...]

    def block_rows(r):
        return slice(r * blk, (r + 1) * blk)

    def scores(r):
        n = (step * MIXER_BLOCKS + r) % nb
        pos = pl.ds(pl.multiple_of(n * blk, blk), blk)
        return n, _swa_scores(r, p_ref.at[block_rows(r)], cos_ref[pos, :], sin_ref[pos, :], kslots_ref, vtslots_ref,
                              **heads)

    def mix(r, z_prev, n, s_bands):
        rows = block_rows(r)
        z_last = _swa_mix(r, n, s_bands, z_prev, p_ref.at[rows], sink_ref, cw_ref, a_sc, vtslots_ref, **heads)
        if r == MIXER_BLOCKS - 1:
            a_sc[rows, :] = jnp.where(tail_step, tail_ref[...], a_sc[rows, :])
        return z_last

    z_last = mix(0, zprev_ref[...], *scores(0))
    for r in range(MIXER_BLOCKS):
        if r + 1 < MIXER_BLOCKS:
            upcoming = scores(r + 1)
        _project_rows(a_sc, block_rows(r), w_ref, o_ref)
        if r + 1 < MIXER_BLOCKS:
            z_last = mix(r + 1, z_last, *upcoming)
        _finish_rows(block_rows(r), x_ref, gp_ref, gn_ref, o_ref, hn_ref)
    znext_ref[...] = z_last


def _swa_out(p, cos, sin, sinks, conv_w, layer, tail, w, x, g_post, g_post_layer, g_next, g_next_layer, *,
             blk, n_heads, n_kv):
    pw = p.shape[1]
    seq = cos.shape[0]
    kv_w = n_kv * HEAD_DIM
    conv_dim = conv_w.shape[-1]
    tm = MIXER_BLOCKS * blk
    table = pl.BlockSpec((seq, LANES), lambda s: (0, 0), pipeline_mode=pl.Buffered(1))
    mix_specs = [
        pl.BlockSpec((tm, pw), lambda s: (s, 0)),
        table, table,
        pl.BlockSpec(memory_space=pltpu.SMEM),
        pl.BlockSpec((None, CONV_WIDTH, conv_dim), lambda s: (layer, 0, 0)),
    ]
    return _mix_project_call(
        functools.partial(_swa_out_kernel, n_heads=n_heads, n_kv=n_kv, blk=blk),
        (p, cos, sin, sinks, conv_w), mix_specs, tail, w, x, g_post, g_post_layer, g_next, g_next_layer,
        blk=blk,
        scratch_shapes=[pltpu.VMEM(((MIXER_BLOCKS + 1) * blk, kv_w), BF16),
                        pltpu.VMEM((kv_w, (MIXER_BLOCKS + 1) * blk), BF16),
                        pltpu.VMEM((SUBLANES, conv_dim), F32), pltpu.VMEM((SUBLANES, conv_dim), F32)],
        mix_bytes=_nbytes((tm, pw), BF16) + _nbytes((seq, LANES), F32),
        name="swa_out")


def _swa_last_rows_kernel(p_ref, cos_ref, sin_ref, kwin_ref, vwin_ref, ztail_ref, *, n_heads, n_kv):
    blk = p_ref.shape[0]
    q_w = n_heads * HEAD_DIM
    kv_w = n_kv * HEAD_DIM
    kwin_ref[...] = _rotated_keys(p_ref, cos_ref[...], sin_ref[...], n_heads=n_heads, n_kv=n_kv)
    vwin_ref[...] = p_ref[:, q_w + kv_w:q_w + 2 * kv_w].astype(F32)
    _, gate_c, h_conv = _conv_gates(p_ref, n_heads=n_heads, n_kv=n_kv, conv_dim=ztail_ref.shape[1])
    ztail_ref[...] = (gate_c * h_conv)[blk - SUBLANES:, :]


def _swa_last_rows(p, cos, sin, *, batch, blk, n_heads, n_kv, conv_dim):
    pw = p.shape[1]
    nb = cos.shape[0] // blk
    kv_w = n_kv * HEAD_DIM
    table = pl.BlockSpec((blk, LANES), lambda b: (nb - 1, 0))
    per_sequence = lambda b: (b, 0, 0)
    return pl.pallas_call(
        functools.partial(_swa_last_rows_kernel, n_heads=n_heads, n_kv=n_kv),
        grid=(batch,),
        in_specs=[pl.BlockSpec((blk, pw), lambda b: (b * nb + nb - 1, 0)), table, table],
        out_specs=[pl.BlockSpec((None, blk, kv_w), per_sequence), pl.BlockSpec((None, blk, kv_w), per_sequence),
                   pl.BlockSpec((None, SUBLANES, conv_dim), per_sequence)],
        out_shape=[jax.ShapeDtypeStruct((batch, blk, kv_w), F32), jax.ShapeDtypeStruct((batch, blk, kv_w), F32),
                   jax.ShapeDtypeStruct((batch, SUBLANES, conv_dim), F32)],
        compiler_params=pltpu.CompilerParams(dimension_semantics=("arbitrary",)),
        name="swa_last_rows",
    )(p, cos, sin)


def _swa_conv_step_kernel(q_ref, kv_ref, gates_ref, ck_ref, cv_ref, st_ref, cos_ref, sin_ref, sink_ref, cw_ref,
                          attn_ref, conv_ref, kout_ref, vout_ref, stout_ref, *, n_heads, n_kv):
    nb, window, kv_w = ck_ref.shape
    gqa = n_heads // n_kv
    row_h = lax.broadcasted_iota(jnp.int32, (n_heads, kv_w), 0)
    lane_h = lax.broadcasted_iota(jnp.int32, (n_heads, kv_w), 1)
    own = (lane_h // HEAD_DIM) == (row_h // gqa)
    key_pos = lax.broadcasted_iota(jnp.int32, (n_heads, window), 1)
    cos = cos_ref[...]
    sin = sin_ref[...]
    sink = sink_ref[...]
    scale = HEAD_DIM ** -0.5
    for i in range(nb):
        q = jnp.where(own, _rope(q_ref[i], cos, sin), 0.0)
        k_new = _rope(kv_ref[i, 0:1, :], cos, sin)
        v_new = kv_ref[i, 1:2, :]
        ck = ck_ref[i]
        cv = cv_ref[i]
        s = lax.dot_general(q.astype(BF16), ck.astype(BF16), (((1,), (1,)), ((), ())),
                            preferred_element_type=F32) * scale
        s = jnp.where(key_pos >= 1, s, -jnp.inf)
        s_new = jnp.sum(q * k_new, axis=-1, keepdims=True) * scale
        m = jnp.maximum(jnp.maximum(jnp.max(s, axis=-1, keepdims=True), s_new), sink)
        p = jnp.exp(s - m)
        p_new = jnp.exp(s_new - m)
        denom = jnp.sum(p, axis=-1, keepdims=True) + p_new + jnp.exp(sink - m)
        o = jnp.dot(p.astype(BF16), cv.astype(BF16), preferred_element_type=F32)
        o = o + p_new * v_new
        o = jnp.where(own, o / denom, 0.0)
        folded = o[:, :LANES]
        for c in range(LANES, kv_w, LANES):
            folded = folded + o[:, c:c + LANES]
        for shift in range(HEAD_DIM, LANES, HEAD_DIM):
            folded = folded + pltpu.roll(folded, shift, axis=1)
        attn_ref[i] = folded

        kout_ref[i] = jnp.where(lax.broadcasted_iota(jnp.int32, ck.shape, 0) == window - 1,
                                k_new, pltpu.roll(ck, window - 1, axis=0))
        vout_ref[i] = jnp.where(lax.broadcasted_iota(jnp.int32, cv.shape, 0) == window - 1,
                                v_new, pltpu.roll(cv, window - 1, axis=0))

        gate_b = gates_ref[i, 0:1, :]
        z = gates_ref[i, 1:2, :] * gates_ref[i, 2:3, :]
        conv = cw_ref[0:1, :] * st_ref[i, 0:1, :]
        conv = conv + cw_ref[1:2, :] * st_ref[i, 1:2, :]
        conv = conv + cw_ref[2:3, :] * z
        conv_ref[i] = gate_b * conv
        stout_ref[i, 0:1, :] = st_ref[i, 1:2, :]
        stout_ref[i, 1:2, :] = z


def _swa_conv_step(q_rep, kv_new, gates, cache_k, cache_v, state, cos, sin, sinks, conv_w, layer, *,
                   nb, n_heads, n_kv):
    db, window, kv_w = cache_k.shape[1:]
    conv_dim = conv_w.shape[-1]
    seq = lambda i: (i, 0, 0)
    lay = lambda i: (layer, i, 0, 0)
    return pl.pallas_call(
        functools.partial(_swa_conv_step_kernel, n_heads=n_heads, n_kv=n_kv),
        grid=(db // nb,),
        in_specs=[
            pl.BlockSpec((nb, n_heads, kv_w), seq),
            pl.BlockSpec((nb, 2, kv_w), seq),
            pl.BlockSpec((nb, 3, conv_dim), seq),
            pl.BlockSpec((None, nb, window, kv_w), lay),
            pl.BlockSpec((None, nb, window, kv_w), lay),
            pl.BlockSpec((None, nb, CONV_WIDTH - 1, conv_dim), lay),
            pl.BlockSpec((1, kv_w), lambda i: (0, 0)),
            pl.BlockSpec((1, kv_w), lambda i: (0, 0)),
            pl.BlockSpec((None, n_heads, 1), lambda i: (layer, 0, 0)),
            pl.BlockSpec((None, CONV_WIDTH, conv_dim), lambda i: (layer, 0, 0)),
        ],
        out_specs=[
            pl.BlockSpec((nb, n_heads, LANES), seq),
            pl.BlockSpec((nb, 1, conv_dim), seq),
            pl.BlockSpec((nb, window, kv_w), seq),
            pl.BlockSpec((nb, window, kv_w), seq),
            pl.BlockSpec((nb, CONV_WIDTH - 1, conv_dim), seq),
        ],
        out_shape=[
            jax.ShapeDtypeStruct((db, n_heads, LANES), F32),
            jax.ShapeDtypeStruct((db, 1, conv_dim), F32),
            jax.ShapeDtypeStruct((db, window, kv_w), F32),
            jax.ShapeDtypeStruct((db, window, kv_w), F32),
            jax.ShapeDtypeStruct((db, CONV_WIDTH - 1, conv_dim), F32),
        ],
        compiler_params=pltpu.CompilerParams(dimension_semantics=("arbitrary",)),
        name="swa_conv_step",
    )(q_rep, kv_new, gates, cache_k, cache_v, state, cos, sin, sinks, conv_w)


def _layernorm(v, g, b):
    vc = v - jnp.mean(v, axis=-1, keepdims=True)
    var = jnp.mean(vc * vc, axis=-1, keepdims=True)
    return vc * lax.rsqrt(var + EPS) * g + b


def _cmlp_out_kernel(z_ref, lg_ref, lb_ref, ws_ref, bs_ref, tail_ref, w_ref, x_ref, gp_ref, gn_ref,
                     o_ref, hn_ref, a_sc, wt_ref):
    groups, chunk = ws_ref.shape[:2]
    width = a_sc.shape[1]
    dg = width // groups
    tail_step = pl.program_id(0) == pl.num_programs(0) - 1

    @pl.when(pl.program_id(0) == 0)
    def _():
        row = lax.broadcasted_iota(jnp.int32, (chunk, chunk), 0)
        col = lax.broadcasted_iota(jnp.int32, (chunk, chunk), 1)
        for g in range(groups):
            wt_ref[g] = jnp.where(row >= col, ws_ref[g], 0.0).astype(BF16)

    def block_rows(r):
        return slice(r * chunk, (r + 1) * chunk)

    def mix(r):
        rows = block_rows(r)
        vn_b = _layernorm(z_ref[rows, width:].astype(F32), lg_ref[...], lb_ref[...]).astype(BF16)
        for g in range(groups):
            lanes = slice(g * dg, (g + 1) * dg)
            mixed = jnp.dot(wt_ref[g], vn_b[:, lanes], preferred_element_type=F32) + bs_ref[g]
            a_sc[rows, lanes] = (z_ref[rows, lanes].astype(F32) * mixed).astype(BF16)
        if r == MIXER_BLOCKS - 1:
            a_sc[rows, :] = jnp.where(tail_step, tail_ref[...], a_sc[rows, :])

    mix(0)
    for r in range(MIXER_BLOCKS):
        if r + 1 < MIXER_BLOCKS:
            mix(r + 1)
        _project_rows(a_sc, block_rows(r), w_ref, o_ref)
        _finish_rows(block_rows(r), x_ref, gp_ref, gn_ref, o_ref, hn_ref)


def _cmlp_out(z, ln_g, ln_b, w_s, b_s, layer, tail, w, x, g_post, g_post_layer, g_next, g_next_layer):
    zw = z.shape[1]
    width = zw // 2
    groups, chunk = w_s.shape[1:3]
    tm = MIXER_BLOCKS * chunk
    mix_specs = [
        pl.BlockSpec((tm, zw), lambda s: (s, 0)),
        pl.BlockSpec((None, 1, width), lambda s: (layer, 0, 0)),
        pl.BlockSpec((None, 1, width), lambda s: (layer, 0, 0)),
        pl.BlockSpec((None, groups, chunk, chunk), lambda s: (layer, 0, 0, 0)),
        pl.BlockSpec((None, groups, chunk, 1), lambda s: (layer, 0, 0, 0)),
    ]
    return _mix_project_call(
        _cmlp_out_kernel, (z, ln_g, ln_b, w_s, b_s), mix_specs, tail, w, x,
        g_post, g_post_layer, g_next, g_next_layer,
        blk=chunk, scratch_shapes=[pltpu.VMEM((groups, chunk, chunk), BF16)],
        mix_bytes=_nbytes((tm, zw), BF16) + 2 * _nbytes((groups, chunk, chunk), F32),
        name="cmlp_out")


def _cmlp_last_v_kernel(z_ref, lg_ref, lb_ref, v_ref):
    v_ref[...] = _layernorm(z_ref[:, v_ref.shape[1]:].astype(F32), lg_ref[...], lb_ref[...])


def _cmlp_last_v(z, ln_g, ln_b, layer, *, batch, n_chunks, chunk):
    zw = z.shape[1]
    width = zw // 2
    per_layer = pl.BlockSpec((None, 1, width), lambda b: (layer, 0, 0))
    return pl.pallas_call(
        _cmlp_last_v_kernel,
        grid=(batch,),
        in_specs=[pl.BlockSpec((chunk, zw), lambda b: (b * n_chunks + n_chunks - 1, 0)), per_layer, per_layer],
        out_specs=pl.BlockSpec((None, chunk, width), lambda b: (b, 0, 0)),
        out_shape=jax.ShapeDtypeStruct((batch, chunk, width), F32),
        compiler_params=pltpu.CompilerParams(dimension_semantics=("arbitrary",)),
        name="cmlp_last_v",
    )(z, ln_g, ln_b)


def _cmlp_gate_step_kernel(z_ref, lg_ref, lb_ref, scale_ref, bias_ref, o_ref, v_ref):
    width = o_ref.shape[1]
    vn = _layernorm(z_ref[:, width:], lg_ref[...], lb_ref[...])
    v_ref[...] = vn
    mix = scale_ref[...] * vn + bias_ref[...]
    o_ref[...] = (z_ref[:, :width] * mix).astype(BF16)


def _cmlp_gate_step(z, ln_g, ln_b, scale, bias, layer):
    rows, zw = z.shape
    width = zw // 2
    per_layer = pl.BlockSpec((None, 1, width), lambda i: (layer, 0, 0))
    return pl.pallas_call(
        _cmlp_gate_step_kernel,
        grid=(1,),
        in_specs=[pl.BlockSpec((rows, zw), lambda i: (0, 0)), per_layer, per_layer, per_layer, per_layer],
        out_specs=[pl.BlockSpec((rows, width), lambda i: (0, 0)), pl.BlockSpec((rows, width), lambda i: (0, 0))],
        out_shape=[jax.ShapeDtypeStruct((rows, width), BF16), jax.ShapeDtypeStruct((rows, width), F32)],
        compiler_params=pltpu.CompilerParams(dimension_semantics=("arbitrary",)),
        name="cmlp_gate_step",
    )(z, ln_g, ln_b, scale, bias)


def _tail_block(rows):
    return jnp.pad(rows.astype(BF16), ((0, TAIL_ROWS - rows.shape[0]), (0, 0)))


def kernel(x_prompt, x_sample, cache_k_win, cache_v_win, state_conv, w_in_even, w_out_even, conv_w, attn_sinks,
           w_in_cmlp, w_out_cmlp, ln_v_g, ln_v_b, w_spatial, b_spatial, w_ffn_up, w_ffn_down,
           g_mix_pre, g_mix_post, g_ffn_pre, g_ffn_post):
    batch, seq, d_model = x_prompt.shape
    dec_batch, dec_seq, _ = x_sample.shape
    assert dec_seq == 1, "the decode kernels take one new token per sequence"
    depth = g_mix_pre.shape[0]
    n_even, _, window, n_kv, head_dim = cache_k_win.shape
    assert head_dim == HEAD_DIM
    n_heads = attn_sinks.shape[1]
    q_w = n_heads * HEAD_DIM
    kv_w = n_kv * HEAD_DIM
    conv_dim = conv_w.shape[-1]
    n_odd, groups, chunk, _ = w_spatial.shape
    cmlp_w = w_out_cmlp.shape[1]
    n_prompt = batch * seq
    rows = n_prompt + TAIL_ROWS
    assert dec_batch <= TAIL_ROWS

    gains = [g.reshape(depth, 1, d_model) for g in (g_mix_pre, g_mix_post, g_ffn_pre, g_ffn_post)]
    g_mix_pre3, g_mix_post3, g_ffn_pre3, g_ffn_post3 = gains
    ln_g3 = ln_v_g.reshape(n_odd, 1, cmlp_w)
    ln_b3 = ln_v_b.reshape(n_odd, 1, cmlp_w)
    b_s4 = b_spatial.reshape(n_odd, groups, chunk, 1)
    step_scale = jnp.repeat(w_spatial[:, :, 0, 0], cmlp_w // groups, axis=1).reshape(n_odd, 1, cmlp_w)
    step_bias = jnp.repeat(b_spatial[:, :, 0], cmlp_w // groups, axis=1).reshape(n_odd, 1, cmlp_w)
    sinks3 = attn_sinks.reshape(n_even, n_heads, 1)
    cos_p, sin_p = _rope_tables(jnp.arange(seq))
    cos_s, sin_s = _rope_tables(PAST_LEN + jnp.arange(dec_seq))
    cos_s = jnp.tile(cos_s, (1, kv_w // LANES))
    sin_s = jnp.tile(sin_s, (1, kv_w // LANES))
    cache_k = cache_k_win.reshape(n_even, dec_batch, window, kv_w)
    cache_v = cache_v_win.reshape(n_even, dec_batch, window, kv_w)

    x, h = _stream_rmsnorm(x_prompt.reshape(n_prompt, d_model), x_sample.reshape(dec_batch, d_model), g_mix_pre3, 0)

    kp, vp, cp, up = [], [], [], []
    ks, vs, cs, us = [], [], [], []
    for i in range(depth):
        j = i // 2
        if i % 2 == 0:
            proj, w_out_b = _panel_matmul(h, w_in_even, j, w_out_even, j, act=_identity, tn=768, name="even_in")
            k_p, v_p, z_p = _swa_last_rows(proj, cos_p, sin_p, batch=batch, blk=window, n_heads=n_heads, n_kv=n_kv,
                                           conv_dim=conv_dim)
            kp.append(k_p.reshape(batch, window, n_kv, HEAD_DIM))
            vp.append(v_p.reshape(batch, window, n_kv, HEAD_DIM))
            cp.append(z_p[:, SUBLANES - (CONV_WIDTH - 1):, :])

            proj_s = proj[n_prompt:n_prompt + dec_batch].astype(F32)
            q_rep = jnp.tile(proj_s[:, :q_w].reshape(dec_batch, n_heads, HEAD_DIM), (1, 1, n_kv))
            kv_new = proj_s[:, q_w:q_w + 2 * kv_w].reshape(dec_batch, 2, kv_w)
            gates = proj_s[:, q_w + 2 * kv_w:].reshape(dec_batch, 3, conv_dim)
            attn_s, conv_s, k_s, v_s, c_s = _swa_conv_step(
                q_rep, kv_new, gates, cache_k, cache_v, state_conv, cos_s, sin_s, sinks3, conv_w, j,
                nb=8, n_heads=n_heads, n_kv=n_kv)
            cat_s = jnp.concatenate([attn_s[:, :, :HEAD_DIM].reshape(dec_batch, q_w),
                                     conv_s.reshape(dec_batch, conv_dim)], axis=1)
            ks.append(k_s.reshape(dec_batch, window, n_kv, HEAD_DIM))
            vs.append(v_s.reshape(dec_batch, window, n_kv, HEAD_DIM))
            cs.append(c_s)
            x, h = _swa_out(proj, cos_p, sin_p, attn_sinks[j], conv_w, j, _tail_block(cat_s), w_out_b, x,
                            g_mix_post3, i, g_ffn_pre3, i, blk=window, n_heads=n_heads, n_kv=n_kv)
        else:
            z, w_out_b = _panel_matmul(h, w_in_cmlp, j, w_out_cmlp, j, act=_gelu_exact, tn=1024, name="cmlp_in")
            up.append(_cmlp_last_v(z, ln_g3, ln_b3, j, batch=batch, n_chunks=seq // chunk, chunk=chunk))
            gated_s, v_s = _cmlp_gate_step(z[n_prompt:n_prompt + dec_batch].astype(F32), ln_g3, ln_b3,
                                           step_scale, step_bias, j)
            us.append(v_s.reshape(dec_batch, dec_seq, cmlp_w))
            x, h = _cmlp_out(z, ln_g3, ln_b3, w_spatial, b_s4, j, _tail_block(gated_s), w_out_b, x,
                             g_mix_post3, i, g_ffn_pre3, i)
        hidden, w_down_b = _panel_matmul(h, w_ffn_up, i, w_ffn_down, i, act=_relu_sq, tn=1024, name="ffn_up",
                                         panel_major=True)
        last = i == depth - 1
        outs = _kstream_matmul(hidden, w_down_b, x, g_ffn_post3, i, g_mix_pre3, min(i + 1, depth - 1),
                               emit_next=not last, name="ffn_down")
        x, h = (outs[0], None) if last else outs

    return (x[:n_prompt].reshape(batch, seq, d_model), x[n_prompt:n_prompt + dec_batch].reshape(dec_batch, dec_seq, d_model),
            jnp.stack(kp), jnp.stack(vp), jnp.stack(cp), jnp.stack(up),
            jnp.stack(ks), jnp.stack(vs), jnp.stack(cs), jnp.stack(us))
```

```python
import functools
import math

import jax
import jax.numpy as jnp
from jax import lax
from jax.experimental import pallas as pl
from jax.experimental.pallas import tpu as pltpu

F32 = jnp.float32
BF16 = jnp.bfloat16

EPS = 1e-6
HEAD_DIM = 64
ROPE_THETA = 10000.0
PAST_LEN = 16384
CONV_WIDTH = 3

LANES = 128
SUBLANES = 8
BF16_ROWS = 16
MXU_COLS = 256
VMEM_BYTES_V7X = 64 * 1024 * 1024

TAIL_ROWS = 128
PANEL_ROW_TILES = 4
KSTREAM_ROW_TILES = 8
EPILOGUE_ROW_BLOCKS = 5
MIXER_BLOCKS = 5
LOG2_E = math.log2(math.e)


def _vmem_limit(block_bytes, temp_bytes):
    need = 2 * block_bytes + temp_bytes + (4 << 20)
    return int(min(need, VMEM_BYTES_V7X - (6 << 20)))


def _nbytes(shape, dtype):
    return math.prod(shape) * jnp.dtype(dtype).itemsize


def _rms_scale(x, g):
    ms = jnp.mean(x * x, axis=-1, keepdims=True)
    return x * lax.rsqrt(ms + EPS) * g


def _identity(y):
    return y


def _gelu_exact(y):
    return 0.5 * y * (1.0 + lax.erf(y * math.sqrt(0.5)))


def _relu_sq(y):
    r = jnp.maximum(y, 0.0)
    return r * r


def _stream_rmsnorm_kernel(*refs):
    xp_refs = refs[:MIXER_BLOCKS]
    xs_ref, g_ref, xo_ref, h_ref = refs[MIXER_BLOCKS:]
    tail_step = pl.program_id(0) == pl.num_programs(0) - 1
    for r, xp_ref in enumerate(xp_refs):
        x = xp_ref[...]
        if r == MIXER_BLOCKS - 1:
            xs = xs_ref[...]
            tail = jnp.concatenate([xs, jnp.zeros((TAIL_ROWS - xs.shape[0], xs.shape[1]), xs.dtype)], axis=0)
            x = jnp.where(tail_step, tail, x)
        rows = slice(r * TAIL_ROWS, (r + 1) * TAIL_ROWS)
        xo_ref[rows, :] = x
        h_ref[rows, :] = _rms_scale(x, g_ref[...]).astype(BF16)


def _stream_rmsnorm(xp, xs, g, g_layer):
    n_prompt, d = xp.shape
    ns = xs.shape[0]
    n_blocks = n_prompt // TAIL_ROWS
    rows = n_prompt + TAIL_ROWS
    tm = MIXER_BLOCKS * TAIL_ROWS
    assert n_prompt % TAIL_ROWS == 0 and rows % tm == 0 and ns % SUBLANES == 0 and ns <= TAIL_ROWS
    prompt_block = lambda r: pl.BlockSpec(
        (None, TAIL_ROWS, d), lambda s: (jnp.minimum(s * MIXER_BLOCKS + r, n_blocks - 1), 0, 0))
    row_tile = pl.BlockSpec((tm, d), lambda s: (s, 0))
    blocks = 2 * _nbytes((tm, d), F32) + _nbytes((tm, d), BF16)
    return pl.pallas_call(
        _stream_rmsnorm_kernel,
        grid=(rows // tm,),
        in_specs=[prompt_block(r) for r in range(MIXER_BLOCKS)] + [
            pl.BlockSpec((ns, d), lambda s: (0, 0)),
            pl.BlockSpec((None, 1, d), lambda s: (g_layer, 0, 0))],
        out_specs=[row_tile, row_tile],
        out_shape=[jax.ShapeDtypeStruct((rows, d), F32), jax.ShapeDtypeStruct((rows, d), BF16)],
        compiler_params=pltpu.CompilerParams(
            dimension_semantics=("arbitrary",), vmem_limit_bytes=_vmem_limit(blocks, 4 << 20)),
        name="stream_rmsnorm",
    )(*([xp.reshape(n_blocks, TAIL_ROWS, d)] * MIXER_BLOCKS), xs, g)


def _panel_matmul_kernel(h_ref, w_ref, side_ref, o_ref, side_o_ref, wb_ref, *, act):
    def step(cast_panel):
        side_o_ref[...] = side_ref[...].astype(BF16)
        for c in range(0, o_ref.shape[1], MXU_COLS):
            cols = slice(c, c + MXU_COLS)
            if cast_panel:
                wb_ref[:, cols] = w_ref[:, cols].astype(BF16)
            y = jnp.dot(h_ref[...], wb_ref[:, cols], preferred_element_type=F32)
            o_ref[:, cols] = act(y).astype(o_ref.dtype)

    @pl.when(pl.program_id(1) == 0)
    def _():
        step(True)

    @pl.when(pl.program_id(1) > 0)
    def _():
        step(False)


def _panel_matmul(h, w, w_layer, side, side_layer, *, act, tn, name, panel_major=False):
    m, k = h.shape
    n = w.shape[-1]
    k2, n2 = side.shape[1:]
    tm = m // PANEL_ROW_TILES
    steps = (n // tn) * PANEL_ROW_TILES
    n_slabs = min(steps, k2 // LANES)
    slab = k2 // n_slabs
    assert m % (PANEL_ROW_TILES * BF16_ROWS) == 0 and n % tn == 0 and tn % MXU_COLS == 0
    assert k2 % n_slabs == 0 and slab % BF16_ROWS == 0
    blocks = (_nbytes((tm, k), BF16) + _nbytes((k, tn), F32) + _nbytes((tm, tn), BF16)
              + _nbytes((slab, n2), F32) + _nbytes((slab, n2), BF16))
    temps = _nbytes((k, tn), BF16) + 3 * _nbytes((tm, MXU_COLS), F32)
    slab_index = lambda j, i: jnp.minimum(j * PANEL_ROW_TILES + i, n_slabs - 1)
    if panel_major:
        out_spec = pl.BlockSpec((None, tm, tn), lambda j, i: (j, i, 0))
        out_shape = jax.ShapeDtypeStruct((n // tn, m, tn), BF16)
    else:
        out_spec = pl.BlockSpec((tm, tn), lambda j, i: (i, j))
        out_shape = jax.ShapeDtypeStruct((m, n), BF16)
    return pl.pallas_call(
        functools.partial(_panel_matmul_kernel, act=act),
        grid=(n // tn, PANEL_ROW_TILES),
        in_specs=[
            pl.BlockSpec((tm, k), lambda j, i: (i, 0)),
            pl.BlockSpec((None, k, tn), lambda j, i: (w_layer, 0, j)),
            pl.BlockSpec((None, slab, n2), lambda j, i: (side_layer, slab_index(j, i), 0)),
        ],
        out_specs=[out_spec, pl.BlockSpec((slab, n2), lambda j, i: (slab_index(j, i), 0))],
        out_shape=[out_shape, jax.ShapeDtypeStruct((k2, n2), BF16)],
        scratch_shapes=[pltpu.VMEM((k, tn), BF16)],
        compiler_params=pltpu.CompilerParams(
            dimension_semantics=("arbitrary", "arbitrary"),
            vmem_limit_bytes=_vmem_limit(blocks, temps)),
        name=name,
    )(h, w, side)


def _kstream_matmul_kernel(a_ref, w_ref, xk_ref, gp_ref, gn_ref, o_ref, *rest, nk, row_blocks, emit_next):
    if emit_next:
        hn_ref, x_sc = rest
    else:
        (x_sc,) = rest
    k = pl.program_id(1)
    tm, n = o_ref.shape
    rb = tm // row_blocks
    for c in range(row_blocks):
        @pl.when(k == c)
        def _(c=c):
            x_sc[c * rb:(c + 1) * rb, :] = xk_ref[...]

    def accumulate(rows, assign):
        a = a_ref[rows, :]
        for c in range(0, n, 2 * MXU_COLS):
            cols = slice(c, c + 2 * MXU_COLS)
            part = jnp.dot(a, w_ref[:, cols], preferred_element_type=F32)
            if assign:
                o_ref[rows, cols] = part
            else:
                o_ref[rows, cols] += part

    @pl.when(k == 0)
    def _():
        accumulate(slice(None), True)

    @pl.when((k > 0) & (k < nk - 1))
    def _():
        accumulate(slice(None), False)

    @pl.when(k == nk - 1)
    def _():
        for r in range(0, tm, rb):
            rows = slice(r, r + rb)
            accumulate(rows, False)
            y = x_sc[rows, :] + _rms_scale(o_ref[rows, :], gp_ref[...])
            o_ref[rows, :] = y
            if emit_next:
                hn_ref[rows, :] = _rms_scale(y, gn_ref[...]).astype(BF16)


def _kstream_matmul(a, w, x, g_post, g_post_layer, g_next, g_next_layer, *, emit_next, name,
                    tm=None, n_tiles=KSTREAM_ROW_TILES, first_tile=0, row_blocks=EPILOGUE_ROW_BLOCKS):
    nk, m, tk = a.shape
    n = w.shape[-1]
    tm = m // n_tiles if tm is None else tm
    rb = tm // row_blocks
    assert (first_tile + n_tiles) * tm <= m and w.shape[0] == nk * tk and nk >= row_blocks
    assert tm % (row_blocks * BF16_ROWS) == 0 and n % (2 * MXU_COLS) == 0
    blocks = (_nbytes((tm, tk), BF16) + _nbytes((tk, n), BF16) + _nbytes((rb, n), F32)
              + _nbytes((tm, n), F32) + (_nbytes((tm, n), BF16) if emit_next else 0))
    temps = _nbytes((tm, n), F32) + 4 * _nbytes((rb, n), F32)
    out_specs = [pl.BlockSpec((tm, n), lambda i, k: (i, 0))]
    out_shape = [jax.ShapeDtypeStruct((n_tiles * tm, n), F32)]
    if emit_next:
        out_specs.append(pl.BlockSpec((tm, n), lambda i, k: (i, 0)))
        out_shape.append(jax.ShapeDtypeStruct((n_tiles * tm, n), BF16))
    return pl.pallas_call(
        functools.partial(_kstream_matmul_kernel, nk=nk, row_blocks=row_blocks, emit_next=emit_next),
        grid=(n_tiles, nk),
        in_specs=[
            pl.BlockSpec((None, tm, tk), lambda i, k: (k, first_tile + i, 0)),
            pl.BlockSpec((tk, n), lambda i, k: (k, 0)),
            pl.BlockSpec((rb, n), lambda i, k: ((first_tile + i) * row_blocks + jnp.minimum(k, row_blocks - 1), 0)),
            pl.BlockSpec((None, 1, n), lambda i, k: (g_post_layer, 0, 0)),
            pl.BlockSpec((None, 1, n), lambda i, k: (g_next_layer, 0, 0)),
        ],
        out_specs=out_specs,
        out_shape=out_shape,
        scratch_shapes=[pltpu.VMEM((tm, n), F32)],
        compiler_params=pltpu.CompilerParams(
            dimension_semantics=("arbitrary", "arbitrary"),
            vmem_limit_bytes=_vmem_limit(blocks, temps)),
        name=name,
    )(a, w, x, g_post, g_next)


def _project_rows(a_sc, rows, w_ref, o_ref):
    for c in range(0, o_ref.shape[1], 2 * MXU_COLS):
        cols = slice(c, c + 2 * MXU_COLS)
        o_ref[rows, cols] = jnp.dot(a_sc[rows, :], w_ref[:, cols], preferred_element_type=F32)


def _finish_rows(rows, x_ref, gp_ref, gn_ref, o_ref, hn_ref):
    y = x_ref[rows, :] + _rms_scale(o_ref[rows, :], gp_ref[...])
    o_ref[rows, :] = y
    hn_ref[rows, :] = _rms_scale(y, gn_ref[...]).astype(BF16)


def _mix_project_call(kernel_fn, mix_inputs, mix_specs, tail, w, x, g_post, g_post_layer, g_next, g_next_layer, *,
                      blk, scratch_shapes, mix_bytes, name):
    m, n = x.shape
    kdim = w.shape[0]
    tm = MIXER_BLOCKS * blk
    assert m % tm == 0 and n % (2 * MXU_COLS) == 0 and tail.shape == (blk, kdim)
    blocks = mix_bytes + _nbytes((blk, kdim), BF16) + 2 * _nbytes((tm, n), F32) + _nbytes((tm, n), BF16)
    temps = _nbytes((kdim, n), BF16) + _nbytes((tm, kdim), BF16) + 4 * _nbytes((blk, n), F32) + (4 << 20)
    row_tile = lambda cols: pl.BlockSpec((tm, cols), lambda s: (s, 0))
    return pl.pallas_call(
        kernel_fn,
        grid=(m // tm,),
        in_specs=list(mix_specs) + [
            pl.BlockSpec((blk, kdim), lambda s: (0, 0)),
            pl.BlockSpec((kdim, n), lambda s: (0, 0), pipeline_mode=pl.Buffered(1)),
            row_tile(n),
            pl.BlockSpec((None, 1, n), lambda s: (g_post_layer, 0, 0)),
            pl.BlockSpec((None, 1, n), lambda s: (g_next_layer, 0, 0)),
        ],
        out_specs=[row_tile(n), row_tile(n)],
        out_shape=[jax.ShapeDtypeStruct((m, n), F32), jax.ShapeDtypeStruct((m, n), BF16)],
        scratch_shapes=[pltpu.VMEM((tm, kdim), BF16)] + list(scratch_shapes),
        compiler_params=pltpu.CompilerParams(
            dimension_semantics=("arbitrary",), vmem_limit_bytes=_vmem_limit(blocks, temps)),
        name=name,
    )(*mix_inputs, tail, w, x, g_post, g_next)


def _rope_tables(positions):
    half = HEAD_DIM // 2
    inv_freq = ROPE_THETA ** (-jnp.arange(half, dtype=F32) / half)
    ang = positions.astype(F32)[:, None] * inv_freq[None, :]
    cos = jnp.cos(ang)
    sin = jnp.sin(ang)
    reps = LANES // HEAD_DIM
    cos_t = jnp.tile(jnp.concatenate([cos, cos], axis=1), (1, reps))
    sin_t = jnp.tile(jnp.concatenate([-sin, sin], axis=1), (1, reps))
    return cos_t, sin_t


def _rope(x, cos, sin):
    half = HEAD_DIM // 2
    w = x.shape[1]
    lane = lax.broadcasted_iota(jnp.int32, x.shape, 1)
    first_half = (lane % HEAD_DIM) < half
    rot = jnp.where(first_half, pltpu.roll(x, w - half, axis=1), pltpu.roll(x, half, axis=1))
    return x * cos + rot * sin


def _rotated_keys(p_ref, cos, sin, *, n_heads, n_kv):
    q_w = n_heads * HEAD_DIM
    kv_w = n_kv * HEAD_DIM
    return _rope(p_ref[:, q_w:q_w + kv_w].astype(F32),
                 jnp.tile(cos, (1, kv_w // LANES)), jnp.tile(sin, (1, kv_w // LANES)))


def _swa_scores(r, p_ref, cos, sin, kslots_ref, vtslots_ref, *, n_heads, n_kv):
    blk = p_ref.shape[0]
    q_w = n_heads * HEAD_DIM
    kv_w = n_kv * HEAD_DIM
    gqa = n_heads // n_kv
    own_slot = slice((r + 1) * blk, (r + 2) * blk)
    band = slice(r * blk, (r + 2) * blk)
    kslots_ref[own_slot, :] = _rotated_keys(p_ref, cos, sin, n_heads=n_heads, n_kv=n_kv).astype(BF16)
    vtslots_ref[:, own_slot] = p_ref[:, q_w + kv_w:q_w + 2 * kv_w].astype(F32).T.astype(BF16)
    q_scale = HEAD_DIM ** -0.5 * LOG2_E
    cos_q = cos * q_scale
    sin_q = sin * q_scale
    k_heads = [kslots_ref[band, kv * HEAD_DIM:(kv + 1) * HEAD_DIM] for kv in range(n_kv)]
    heads_per_group = LANES // HEAD_DIM
    s_bands = []
    for pair in range(q_w // LANES):
        q_rot = _rope(p_ref[:, pair * LANES:(pair + 1) * LANES].astype(F32), cos_q, sin_q).astype(BF16)
        for sub in range(heads_per_group):
            kv = (pair * heads_per_group + sub) // gqa
            q_h = q_rot[:, sub * HEAD_DIM:(sub + 1) * HEAD_DIM]
            s_bands.append(lax.dot_general(k_heads[kv], q_h, (((1,), (1,)), ((), ())),
                                           preferred_element_type=F32))
    return s_bands


def _conv_gates(p_ref, *, n_heads, n_kv, conv_dim):
    g_off = (n_heads + 2 * n_kv) * HEAD_DIM
    return [p_ref[:, g_off + i * conv_dim:g_off + (i + 1) * conv_dim].astype(F32) for i in range(CONV_WIDTH)]


def _swa_mix(r, n, s_bands, z_prev, p_ref, sink_ref, cw_ref, out_ref, vtslots_ref, *, n_heads, n_kv):
    blk = p_ref.shape[0]
    q_w = n_heads * HEAD_DIM
    conv_dim = cw_ref.shape[1]
    gqa = n_heads // n_kv
    heads_per_group = LANES // HEAD_DIM
    rows = slice(r * blk, (r + 1) * blk)
    band = slice(r * blk, (r + 2) * blk)
    key = lax.broadcasted_iota(jnp.int32, (blk, blk), 0)
    qry = lax.broadcasted_iota(jnp.int32, (blk, blk), 1)
    own = key <= qry
    no_prev = jnp.where(own | (n > 0), 0.0, -jnp.inf)
    p_bands, inv_denoms = [], []
    for h, s_band in enumerate(s_bands):
        s = jnp.where(own, s_band[blk:], s_band[:blk]) + no_prev
        sink = sink_ref[h] * LOG2_E
        m = jnp.maximum(jnp.max(s, axis=0, keepdims=True), sink)
        p = jnp.exp2(s - m)
        denom = jnp.sum(p, axis=0, keepdims=True) + jnp.exp2(sink - m)
        p_bands.append(jnp.concatenate([jnp.where(own, 0.0, p), jnp.where(own, p, 0.0)], axis=0).astype(BF16))
        inv_denoms.append(1.0 / denom)
    for pair in range(q_w // LANES):
        outs_t = []
        for sub in range(heads_per_group):
            h = pair * heads_per_group + sub
            kv = h // gqa
            o_t = jnp.dot(vtslots_ref[kv * HEAD_DIM:(kv + 1) * HEAD_DIM, band], p_bands[h],
                          preferred_element_type=F32)
            outs_t.append(o_t * inv_denoms[h])
        out_ref[rows, pair * LANES:(pair + 1) * LANES] = jnp.concatenate(outs_t, axis=0).T.astype(BF16)

    gate_b, gate_c, h_conv = _conv_gates(p_ref, n_heads=n_heads, n_kv=n_kv, conv_dim=conv_dim)
    z = gate_c * h_conv
    z_prev = jnp.where(n > 0, z_prev, 0.0)
    top = lax.broadcasted_iota(jnp.int32, z_prev.shape, 0)

    def delayed(d):
        rolled = pltpu.roll(z, d, axis=0)
        head = jnp.where(top < d, pltpu.roll(z_prev, d, axis=0), rolled[:SUBLANES])
        return jnp.concatenate([head, rolled[SUBLANES:]], axis=0)

    conv = cw_ref[0:1, :] * delayed(2)
    conv = conv + cw_ref[1:2, :] * delayed(1)
    conv = conv + cw_ref[2:3, :] * z
    out_ref[rows, q_w:] = (gate_b * conv).astype(BF16)
    return z[blk - SUBLANES:, :]


def _swa_out_kernel(p_ref, cos_ref, sin_ref, sink_ref, cw_ref, tail_ref, w_ref, x_ref, gp_ref, gn_ref,
                    o_ref, hn_ref, a_sc, kslots_ref, vtslots_ref, zprev_ref, znext_ref, *, n_heads, n_kv, blk):
    nb = cos_ref.shape[0] // blk
    step = pl.program_id(0)
    tail_step = step == pl.num_programs(0) - 1
    heads = dict(n_heads=n_heads, n_kv=n_kv)
    first_slot = slice(0, blk)
    last_slot = slice(MIXER_BLOCKS * blk, (MIXER_BLOCKS + 1) * blk)

    @pl.when(step == 0)
    def _():
        kslots_ref[first_slot, :] = jnp.zeros((blk, kslots_ref.shape[1]), BF16)
        vtslots_ref[:, first_slot] = jnp.zeros((vtslots_ref.shape[0], blk), BF16)
        zprev_ref[...] = jnp.zeros_like(zprev_ref)

    @pl.when(step > 0)
    def _():
        kslots_ref[first_slot, :] = kslots_ref[last_slot, :]
        vtslots_ref[:, first_slot] = vtslots_ref[:, last_slot]
        zprev_ref[...] = znext_ref[...]

    def block_rows(r):
        return slice(r * blk, (r + 1) * blk)

    def scores(r):
        n = (step * MIXER_BLOCKS + r) % nb
        pos = pl.ds(pl.multiple_of(n * blk, blk), blk)
        return n, _swa_scores(r, p_ref.at[block_rows(r)], cos_ref[pos, :], sin_ref[pos, :], kslots_ref, vtslots_ref,
                              **heads)

    def mix(r, z_prev, n, s_bands):
        rows = block_rows(r)
        z_last = _swa_mix(r, n, s_bands, z_prev, p_ref.at[rows], sink_ref, cw_ref, a_sc, vtslots_ref, **heads)
        if r == MIXER_BLOCKS - 1:
            a_sc[rows, :] = jnp.where(tail_step, tail_ref[...], a_sc[rows, :])
        return z_last

    z_last = mix(0, zprev_ref[...], *scores(0))
    for r in range(MIXER_BLOCKS):
        if r + 1 < MIXER_BLOCKS:
            upcoming = scores(r + 1)
        _project_rows(a_sc, block_rows(r), w_ref, o_ref)
        if r + 1 < MIXER_BLOCKS:
            z_last = mix(r + 1, z_last, *upcoming)
        _finish_rows(block_rows(r), x_ref, gp_ref, gn_ref, o_ref, hn_ref)
    znext_ref[...] = z_last


def _swa_out(p, cos, sin, sinks, conv_w, layer, tail, w, x, g_post, g_post_layer, g_next, g_next_layer, *,
             blk, n_heads, n_kv):
    pw = p.shape[1]
    seq = cos.shape[0]
    kv_w = n_kv * HEAD_DIM
    conv_dim = conv_w.shape[-1]
    tm = MIXER_BLOCKS * blk
    table = pl.BlockSpec((seq, LANES), lambda s: (0, 0), pipeline_mode=pl.Buffered(1))
    mix_specs = [
        pl.BlockSpec((tm, pw), lambda s: (s, 0)),
        table, table,
        pl.BlockSpec(memory_space=pltpu.SMEM),
        pl.BlockSpec((None, CONV_WIDTH, conv_dim), lambda s: (layer, 0, 0)),
    ]
    return _mix_project_call(
        functools.partial(_swa_out_kernel, n_heads=n_heads, n_kv=n_kv, blk=blk),
        (p, cos, sin, sinks, conv_w), mix_specs, tail, w, x, g_post, g_post_layer, g_next, g_next_layer,
        blk=blk,
        scratch_shapes=[pltpu.VMEM(((MIXER_BLOCKS + 1) * blk, kv_w), BF16),
                        pltpu.VMEM((kv_w, (MIXER_BLOCKS + 1) * blk), BF16),
                        pltpu.VMEM((SUBLANES, conv_dim), F32), pltpu.VMEM((SUBLANES, conv_dim), F32)],
        mix_bytes=_nbytes((tm, pw), BF16) + _nbytes((seq, LANES), F32),
        name="swa_out")


def _swa_last_rows_kernel(p_ref, cos_ref, sin_ref, kwin_ref, vwin_ref, ztail_ref, *, n_heads, n_kv):
    blk = p_ref.shape[0]
    q_w = n_heads * HEAD_DIM
    kv_w = n_kv * HEAD_DIM
    kwin_ref[...] = _rotated_keys(p_ref, cos_ref[...], sin_ref[...], n_heads=n_heads, n_kv=n_kv)
    vwin_ref[...] = p_ref[:, q_w + kv_w:q_w + 2 * kv_w].astype(F32)
    _, gate_c, h_conv = _conv_gates(p_ref, n_heads=n_heads, n_kv=n_kv, conv_dim=ztail_ref.shape[1])
    ztail_ref[...] = (gate_c * h_conv)[blk - SUBLANES:, :]


def _swa_last_rows(p, cos, sin, *, batch, blk, n_heads, n_kv, conv_dim):
    pw = p.shape[1]
    nb = cos.shape[0] // blk
    kv_w = n_kv * HEAD_DIM
    table = pl.BlockSpec((blk, LANES), lambda b: (nb - 1, 0))
    per_sequence = lambda b: (b, 0, 0)
    return pl.pallas_call(
        functools.partial(_swa_last_rows_kernel, n_heads=n_heads, n_kv=n_kv),
        grid=(batch,),
        in_specs=[pl.BlockSpec((blk, pw), lambda b: (b * nb + nb - 1, 0)), table, table],
        out_specs=[pl.BlockSpec((None, blk, kv_w), per_sequence), pl.BlockSpec((None, blk, kv_w), per_sequence),
                   pl.BlockSpec((None, SUBLANES, conv_dim), per_sequence)],
        out_shape=[jax.ShapeDtypeStruct((batch, blk, kv_w), F32), jax.ShapeDtypeStruct((batch, blk, kv_w), F32),
                   jax.ShapeDtypeStruct((batch, SUBLANES, conv_dim), F32)],
        compiler_params=pltpu.CompilerParams(dimension_semantics=("arbitrary",)),
        name="swa_last_rows",
    )(p, cos, sin)


def _swa_conv_step_kernel(q_ref, kv_ref, gates_ref, ck_ref, cv_ref, st_ref, cos_ref, sin_ref, sink_ref, cw_ref,
                          attn_ref, conv_ref, kout_ref, vout_ref, stout_ref, *, n_heads, n_kv):
    nb, window, kv_w = ck_ref.shape
    gqa = n_heads // n_kv
    row_h = lax.broadcasted_iota(jnp.int32, (n_heads, kv_w), 0)
    lane_h = lax.broadcasted_iota(jnp.int32, (n_heads, kv_w), 1)
    own = (lane_h // HEAD_DIM) == (row_h // gqa)
    key_pos = lax.broadcasted_iota(jnp.int32, (n_heads, window), 1)
    cos = cos_ref[...]
    sin = sin_ref[...]
    sink = sink_ref[...]
    scale = HEAD_DIM ** -0.5
    newest = lax.broadcasted_iota(jnp.int32, (window, kv_w), 0) == window - 1
    staged = []
    for i in range(nb):
        q = jnp.where(own, _rope(q_ref[i], cos, sin), 0.0)
        k_new = _rope(kv_ref[i, 0:1, :], cos, sin)
        s = lax.dot_general(q.astype(BF16), ck_ref[i].astype(BF16), (((1,), (1,)), ((), ())),
                            preferred_element_type=F32) * scale
        s_new = jnp.sum(q * k_new, axis=-1, keepdims=True) * scale
        kout_ref[i] = jnp.where(newest, k_new, pltpu.roll(ck_ref[i], window - 1, axis=0))
        staged.append((s, s_new))
    weights = []
    for s, s_new in staged:
        s = jnp.where(key_pos >= 1, s, -jnp.inf)
        m = jnp.maximum(jnp.maximum(jnp.max(s, axis=-1, keepdims=True), s_new), sink)
        p = jnp.exp(s - m)
        p_new = jnp.exp(s_new - m)
        denom = jnp.sum(p, axis=-1, keepdims=True) + p_new + jnp.exp(sink - m)
        weights.append((p.astype(BF16), p_new, denom))
    for i, (p, p_new, denom) in enumerate(weights):
        v_new = kv_ref[i, 1:2, :]
        o = jnp.dot(p, cv_ref[i].astype(BF16), preferred_element_type=F32)
        o = o + p_new * v_new
        o = jnp.where(own, o / denom, 0.0)
        folded = o[:, :LANES]
        for c in range(LANES, kv_w, LANES):
            folded = folded + o[:, c:c + LANES]
        for shift in range(HEAD_DIM, LANES, HEAD_DIM):
            folded = folded + pltpu.roll(folded, shift, axis=1)
        attn_ref[i] = folded
        vout_ref[i] = jnp.where(newest, v_new, pltpu.roll(cv_ref[i], window - 1, axis=0))

    for i in range(nb):
        gate_b = gates_ref[i, 0:1, :]
        z = gates_ref[i, 1:2, :] * gates_ref[i, 2:3, :]
        conv = cw_ref[0:1, :] * st_ref[i, 0:1, :]
        conv = conv + cw_ref[1:2, :] * st_ref[i, 1:2, :]
        conv = conv + cw_ref[2:3, :] * z
        conv_ref[i] = gate_b * conv
        stout_ref[i, 0:1, :] = st_ref[i, 1:2, :]
        stout_ref[i, 1:2, :] = z


def _swa_conv_step(q_rep, kv_new, gates, cache_k, cache_v, state, cos, sin, sinks, conv_w, layer, *,
                   nb, n_heads, n_kv):
    db, window, kv_w = cache_k.shape[1:]
    conv_dim = conv_w.shape[-1]
    seq = lambda i: (i, 0, 0)
    lay = lambda i: (layer, i, 0, 0)
    return pl.pallas_call(
        functools.partial(_swa_conv_step_kernel, n_heads=n_heads, n_kv=n_kv),
        grid=(db // nb,),
        in_specs=[
            pl.BlockSpec((nb, n_heads, kv_w), seq),
            pl.BlockSpec((nb, 2, kv_w), seq),
            pl.BlockSpec((nb, 3, conv_dim), seq),
            pl.BlockSpec((None, nb, window, kv_w), lay),
            pl.BlockSpec((None, nb, window, kv_w), lay),
            pl.BlockSpec((None, nb, CONV_WIDTH - 1, conv_dim), lay),
            pl.BlockSpec((1, kv_w), lambda i: (0, 0)),
            pl.BlockSpec((1, kv_w), lambda i: (0, 0)),
            pl.BlockSpec((None, n_heads, 1), lambda i: (layer, 0, 0)),
            pl.BlockSpec((None, CONV_WIDTH, conv_dim), lambda i: (layer, 0, 0)),
        ],
        out_specs=[
            pl.BlockSpec((nb, n_heads, LANES), seq),
            pl.BlockSpec((nb, 1, conv_dim), seq),
            pl.BlockSpec((nb, window, kv_w), seq),
            pl.BlockSpec((nb, window, kv_w), seq),
            pl.BlockSpec((nb, CONV_WIDTH - 1, conv_dim), seq),
        ],
        out_shape=[
            jax.ShapeDtypeStruct((db, n_heads, LANES), F32),
            jax.ShapeDtypeStruct((db, 1, conv_dim), F32),
            jax.ShapeDtypeStruct((db, window, kv_w), F32),
            jax.ShapeDtypeStruct((db, window, kv_w), F32),
            jax.ShapeDtypeStruct((db, CONV_WIDTH - 1, conv_dim), F32),
        ],
        compiler_params=pltpu.CompilerParams(dimension_semantics=("arbitrary",)),
        name="swa_conv_step",
    )(q_rep, kv_new, gates, cache_k, cache_v, state, cos, sin, sinks, conv_w)


def _layernorm(v, g, b):
    vc = v - jnp.mean(v, axis=-1, keepdims=True)
    var = jnp.mean(vc * vc, axis=-1, keepdims=True)
    return vc * lax.rsqrt(var + EPS) * g + b


def _cmlp_out_kernel(z_ref, lg_ref, lb_ref, ws_ref, bs_ref, tail_ref, w_ref, x_ref, gp_ref, gn_ref,
                     o_ref, hn_ref, a_sc, wt_ref):
    groups, chunk = ws_ref.shape[:2]
    width = a_sc.shape[1]
    dg = width // groups
    tail_step = pl.program_id(0) == pl.num_programs(0) - 1

    @pl.when(pl.program_id(0) == 0)
    def _():
        row = lax.broadcasted_iota(jnp.int32, (chunk, chunk), 0)
        col = lax.broadcasted_iota(jnp.int32, (chunk, chunk), 1)
        for g in range(groups):
            wt_ref[g] = jnp.where(row >= col, ws_ref[g], 0.0).astype(BF16)

    def block_rows(r):
        return slice(r * chunk, (r + 1) * chunk)

    def mix(r):
        rows = block_rows(r)
        vn_b = _layernorm(z_ref[rows, width:].astype(F32), lg_ref[...], lb_ref[...]).astype(BF16)
        for g in range(groups):
            lanes = slice(g * dg, (g + 1) * dg)
            mixed = jnp.dot(wt_ref[g], vn_b[:, lanes], preferred_element_type=F32) + bs_ref[g]
            a_sc[rows, lanes] = (z_ref[rows, lanes].astype(F32) * mixed).astype(BF16)
        if r == MIXER_BLOCKS - 1:
            a_sc[rows, :] = jnp.where(tail_step, tail_ref[...], a_sc[rows, :])

    mix(0)
    for r in range(MIXER_BLOCKS):
        if r + 1 < MIXER_BLOCKS:
            mix(r + 1)
        _project_rows(a_sc, block_rows(r), w_ref, o_ref)
        _finish_rows(block_rows(r), x_ref, gp_ref, gn_ref, o_ref, hn_ref)


def _cmlp_out(z, ln_g, ln_b, w_s, b_s, layer, tail, w, x, g_post, g_post_layer, g_next, g_next_layer):
    zw = z.shape[1]
    width = zw // 2
    groups, chunk = w_s.shape[1:3]
    tm = MIXER_BLOCKS * chunk
    mix_specs = [
        pl.BlockSpec((tm, zw), lambda s: (s, 0)),
        pl.BlockSpec((None, 1, width), lambda s: (layer, 0, 0)),
        pl.BlockSpec((None, 1, width), lambda s: (layer, 0, 0)),
        pl.BlockSpec((None, groups, chunk, chunk), lambda s: (layer, 0, 0, 0)),
        pl.BlockSpec((None, groups, chunk, 1), lambda s: (layer, 0, 0, 0)),
    ]
    return _mix_project_call(
        _cmlp_out_kernel, (z, ln_g, ln_b, w_s, b_s), mix_specs, tail, w, x,
        g_post, g_post_layer, g_next, g_next_layer,
        blk=chunk, scratch_shapes=[pltpu.VMEM((groups, chunk, chunk), BF16)],
        mix_bytes=_nbytes((tm, zw), BF16) + 2 * _nbytes((groups, chunk, chunk), F32),
        name="cmlp_out")


def _cmlp_last_v_kernel(z_ref, lg_ref, lb_ref, v_ref):
    v_ref[...] = _layernorm(z_ref[:, v_ref.shape[1]:].astype(F32), lg_ref[...], lb_ref[...])


def _cmlp_last_v(z, ln_g, ln_b, layer, *, batch, n_chunks, chunk):
    zw = z.shape[1]
    width = zw // 2
    per_layer = pl.BlockSpec((None, 1, width), lambda b: (layer, 0, 0))
    return pl.pallas_call(
        _cmlp_last_v_kernel,
        grid=(batch,),
        in_specs=[pl.BlockSpec((chunk, zw), lambda b: (b * n_chunks + n_chunks - 1, 0)), per_layer, per_layer],
        out_specs=pl.BlockSpec((None, chunk, width), lambda b: (b, 0, 0)),
        out_shape=jax.ShapeDtypeStruct((batch, chunk, width), F32),
        compiler_params=pltpu.CompilerParams(dimension_semantics=("arbitrary",)),
        name="cmlp_last_v",
    )(z, ln_g, ln_b)


def _cmlp_gate_step_kernel(z_ref, lg_ref, lb_ref, scale_ref, bias_ref, o_ref, v_ref):
    width = o_ref.shape[1]
    vn = _layernorm(z_ref[:, width:], lg_ref[...], lb_ref[...])
    v_ref[...] = vn
    mix = scale_ref[...] * vn + bias_ref[...]
    o_ref[...] = (z_ref[:, :width] * mix).astype(BF16)


def _cmlp_gate_step(z, ln_g, ln_b, scale, bias, layer):
    rows, zw = z.shape
    width = zw // 2
    per_layer = pl.BlockSpec((None, 1, width), lambda i: (layer, 0, 0))
    return pl.pallas_call(
        _cmlp_gate_step_kernel,
        grid=(1,),
        in_specs=[pl.BlockSpec((rows, zw), lambda i: (0, 0)), per_layer, per_layer, per_layer, per_layer],
        out_specs=[pl.BlockSpec((rows, width), lambda i: (0, 0)), pl.BlockSpec((rows, width), lambda i: (0, 0))],
        out_shape=[jax.ShapeDtypeStruct((rows, width), BF16), jax.ShapeDtypeStruct((rows, width), F32)],
        compiler_params=pltpu.CompilerParams(dimension_semantics=("arbitrary",)),
        name="cmlp_gate_step",
    )(z, ln_g, ln_b, scale, bias)


def _tail_block(rows):
    return jnp.pad(rows.astype(BF16), ((0, TAIL_ROWS - rows.shape[0]), (0, 0)))


def kernel(x_prompt, x_sample, cache_k_win, cache_v_win, state_conv, w_in_even, w_out_even, conv_w, attn_sinks,
           w_in_cmlp, w_out_cmlp, ln_v_g, ln_v_b, w_spatial, b_spatial, w_ffn_up, w_ffn_down,
           g_mix_pre, g_mix_post, g_ffn_pre, g_ffn_post):
    batch, seq, d_model = x_prompt.shape
    dec_batch, dec_seq, _ = x_sample.shape
    assert dec_seq == 1, "the decode kernels take one new token per sequence"
    depth = g_mix_pre.shape[0]
    n_even, _, window, n_kv, head_dim = cache_k_win.shape
    assert head_dim == HEAD_DIM
    n_heads = attn_sinks.shape[1]
    q_w = n_heads * HEAD_DIM
    kv_w = n_kv * HEAD_DIM
    conv_dim = conv_w.shape[-1]
    n_odd, groups, chunk, _ = w_spatial.shape
    cmlp_w = w_out_cmlp.shape[1]
    n_prompt = batch * seq
    rows = n_prompt + TAIL_ROWS
    assert dec_batch <= TAIL_ROWS

    gains = [g.reshape(depth, 1, d_model) for g in (g_mix_pre, g_mix_post, g_ffn_pre, g_ffn_post)]
    g_mix_pre3, g_mix_post3, g_ffn_pre3, g_ffn_post3 = gains
    ln_g3 = ln_v_g.reshape(n_odd, 1, cmlp_w)
    ln_b3 = ln_v_b.reshape(n_odd, 1, cmlp_w)
    b_s4 = b_spatial.reshape(n_odd, groups, chunk, 1)
    step_scale = jnp.repeat(w_spatial[:, :, 0, 0], cmlp_w // groups, axis=1).reshape(n_odd, 1, cmlp_w)
    step_bias = jnp.repeat(b_spatial[:, :, 0], cmlp_w // groups, axis=1).reshape(n_odd, 1, cmlp_w)
    sinks3 = attn_sinks.reshape(n_even, n_heads, 1)
    cos_p, sin_p = _rope_tables(jnp.arange(seq))
    cos_s, sin_s = _rope_tables(PAST_LEN + jnp.arange(dec_seq))
    cos_s = jnp.tile(cos_s, (1, kv_w // LANES))
    sin_s = jnp.tile(sin_s, (1, kv_w // LANES))
    cache_k = cache_k_win.reshape(n_even, dec_batch, window, kv_w)
    cache_v = cache_v_win.reshape(n_even, dec_batch, window, kv_w)

    x, h = _stream_rmsnorm(x_prompt.reshape(n_prompt, d_model), x_sample.reshape(dec_batch, d_model), g_mix_pre3, 0)

    kp, vp, cp, up = [], [], [], []
    ks, vs, cs, us = [], [], [], []
    for i in range(depth):
        j = i // 2
        if i % 2 == 0:
            proj, w_out_b = _panel_matmul(h, w_in_even, j, w_out_even, j, act=_identity, tn=768, name="even_in")
            k_p, v_p, z_p = _swa_last_rows(proj, cos_p, sin_p, batch=batch, blk=window, n_heads=n_heads, n_kv=n_kv,
                                           conv_dim=conv_dim)
            kp.append(k_p.reshape(batch, window, n_kv, HEAD_DIM))
            vp.append(v_p.reshape(batch, window, n_kv, HEAD_DIM))
            cp.append(z_p[:, SUBLANES - (CONV_WIDTH - 1):, :])

            proj_s = proj[n_prompt:n_prompt + dec_batch].astype(F32)
            q_rep = jnp.tile(proj_s[:, :q_w].reshape(dec_batch, n_heads, HEAD_DIM), (1, 1, n_kv))
            kv_new = proj_s[:, q_w:q_w + 2 * kv_w].reshape(dec_batch, 2, kv_w)
            gates = proj_s[:, q_w + 2 * kv_w:].reshape(dec_batch, 3, conv_dim)
            attn_s, conv_s, k_s, v_s, c_s = _swa_conv_step(
                q_rep, kv_new, gates, cache_k, cache_v, state_conv, cos_s, sin_s, sinks3, conv_w, j,
                nb=8, n_heads=n_heads, n_kv=n_kv)
            cat_s = jnp.concatenate([attn_s[:, :, :HEAD_DIM].reshape(dec_batch, q_w),
                                     conv_s.reshape(dec_batch, conv_dim)], axis=1)
            ks.append(k_s.reshape(dec_batch, window, n_kv, HEAD_DIM))
            vs.append(v_s.reshape(dec_batch, window, n_kv, HEAD_DIM))
            cs.append(c_s)
            x, h = _swa_out(proj, cos_p, sin_p, attn_sinks[j], conv_w, j, _tail_block(cat_s), w_out_b, x,
                            g_mix_post3, i, g_ffn_pre3, i, blk=window, n_heads=n_heads, n_kv=n_kv)
        else:
            z, w_out_b = _panel_matmul(h, w_in_cmlp, j, w_out_cmlp, j, act=_gelu_exact, tn=1024, name="cmlp_in")
            up.append(_cmlp_last_v(z, ln_g3, ln_b3, j, batch=batch, n_chunks=seq // chunk, chunk=chunk))
            gated_s, v_s = _cmlp_gate_step(z[n_prompt:n_prompt + dec_batch].astype(F32), ln_g3, ln_b3,
                                           step_scale, step_bias, j)
            us.append(v_s.reshape(dec_batch, dec_seq, cmlp_w))
            x, h = _cmlp_out(z, ln_g3, ln_b3, w_spatial, b_s4, j, _tail_block(gated_s), w_out_b, x,
                             g_mix_post3, i, g_ffn_pre3, i)
        hidden, w_down_b = _panel_matmul(h, w_ffn_up, i, w_ffn_down, i, act=_relu_sq, tn=1024, name="ffn_up",
                                         panel_major=True)
        if i + 1 < depth:
            x, h = _kstream_matmul(hidden, w_down_b, x, g_ffn_post3, i, g_mix_pre3, i + 1,
                                   emit_next=True, name="ffn_down")
    last = depth - 1
    prompt_tile = n_prompt // KSTREAM_ROW_TILES
    (y_prompt,) = _kstream_matmul(hidden, w_down_b, x, g_ffn_post3, last, g_ffn_post3, last, emit_next=False,
                                  name="ffn_down_prompt", tm=prompt_tile, row_blocks=4)
    (y_tail,) = _kstream_matmul(hidden, w_down_b, x, g_ffn_post3, last, g_ffn_post3, last, emit_next=False,
                                name="ffn_down_tail", tm=TAIL_ROWS, n_tiles=1, first_tile=n_prompt // TAIL_ROWS,
                                row_blocks=1)

    return (y_prompt.reshape(batch, seq, d_model), y_tail[:dec_batch].reshape(dec_batch, dec_seq, d_model),
            jnp.stack(kp), jnp.stack(vp), jnp.stack(cp), jnp.stack(up),
            jnp.stack(ks), jnp.stack(vs), jnp.stack(cs), jnp.stack(us))
```

```python
import functools
import math

import jax
import jax.numpy as jnp
from jax import lax
from jax.experimental import pallas as pl
from jax.experimental.pallas import tpu as pltpu

F32 = jnp.float32
BF16 = jnp.bfloat16

EPS = 1e-6
HEAD_DIM = 64
ROPE_THETA = 10000.0
PAST_LEN = 16384
CONV_WIDTH = 3

LANES = 128
SUBLANES = 8
BF16_ROWS = 16
MXU_COLS = 256
VMEM_BYTES_V7X = 64 * 1024 * 1024

TAIL_ROWS = 128
PANEL_ROW_TILES = 4
KSTREAM_ROW_TILES = 8
EPILOGUE_ROW_BLOCKS = 5
MIXER_BLOCKS = 5
LOG2_E = math.log2(math.e)


def _vmem_limit(block_bytes, temp_bytes):
    need = 2 * block_bytes + temp_bytes + (4 << 20)
    return int(min(need, VMEM_BYTES_V7X - (6 << 20)))


def _nbytes(shape, dtype):
    return math.prod(shape) * jnp.dtype(dtype).itemsize


def _rms_scale(x, g):
    ms = jnp.mean(x * x, axis=-1, keepdims=True)
    return x * lax.rsqrt(ms + EPS) * g


def _identity(y):
    return y


def _gelu_exact(y):
    return 0.5 * y * (1.0 + lax.erf(y * math.sqrt(0.5)))


def _relu_sq(y):
    r = jnp.maximum(y, 0.0)
    return r * r


def _stream_specs(xp, xs):
    n_prompt, d = xp.shape
    n_blocks = n_prompt // TAIL_ROWS
    assert n_prompt % TAIL_ROWS == 0 and (n_blocks + 1) % MIXER_BLOCKS == 0
    assert xs.shape[0] % SUBLANES == 0 and xs.shape[0] <= TAIL_ROWS
    prompt_block = lambda r: pl.BlockSpec(
        (None, TAIL_ROWS, d), lambda s: (jnp.minimum(s * MIXER_BLOCKS + r, n_blocks - 1), 0, 0))
    specs = [prompt_block(r) for r in range(MIXER_BLOCKS)] + [pl.BlockSpec(xs.shape, lambda s: (0, 0))]
    return [xp.reshape(n_blocks, TAIL_ROWS, d)] * MIXER_BLOCKS + [xs], specs


def _stream_block(stream_refs, r):
    x = stream_refs[r][...]
    if r == MIXER_BLOCKS - 1:
        xs = stream_refs[MIXER_BLOCKS][...]
        tail = jnp.concatenate([xs, jnp.zeros((TAIL_ROWS - xs.shape[0], xs.shape[1]), xs.dtype)], axis=0)
        x = jnp.where(pl.program_id(0) == pl.num_programs(0) - 1, tail, x)
    return x


def _stream_rmsnorm_kernel(*refs):
    g_ref, h_ref = refs[MIXER_BLOCKS + 1:]
    for r in range(MIXER_BLOCKS):
        h_ref[r * TAIL_ROWS:(r + 1) * TAIL_ROWS, :] = _rms_scale(_stream_block(refs, r), g_ref[...]).astype(BF16)


def _stream_rmsnorm(xp, xs, g, g_layer):
    n_prompt, d = xp.shape
    rows = n_prompt + TAIL_ROWS
    tm = MIXER_BLOCKS * TAIL_ROWS
    inputs, specs = _stream_specs(xp, xs)
    blocks = _nbytes((tm, d), F32) + _nbytes((tm, d), BF16)
    return pl.pallas_call(
        _stream_rmsnorm_kernel,
        grid=(rows // tm,),
        in_specs=specs + [pl.BlockSpec((None, 1, d), lambda s: (g_layer, 0, 0))],
        out_specs=pl.BlockSpec((tm, d), lambda s: (s, 0)),
        out_shape=jax.ShapeDtypeStruct((rows, d), BF16),
        compiler_params=pltpu.CompilerParams(
            dimension_semantics=("arbitrary",), vmem_limit_bytes=_vmem_limit(blocks, 4 << 20)),
        name="stream_rmsnorm",
    )(*inputs, g)


def _panel_matmul_kernel(h_ref, w_ref, side_ref, o_ref, side_o_ref, wb_ref, *, act):
    def step(cast_panel):
        side_o_ref[...] = side_ref[...].astype(BF16)
        for c in range(0, o_ref.shape[1], MXU_COLS):
            cols = slice(c, c + MXU_COLS)
            if cast_panel:
                wb_ref[:, cols] = w_ref[:, cols].astype(BF16)
            y = jnp.dot(h_ref[...], wb_ref[:, cols], preferred_element_type=F32)
            o_ref[:, cols] = act(y).astype(o_ref.dtype)

    @pl.when(pl.program_id(1) == 0)
    def _():
        step(True)

    @pl.when(pl.program_id(1) > 0)
    def _():
        step(False)


def _panel_matmul(h, w, w_layer, side, side_layer, *, act, tn, name, panel_major=False):
    m, k = h.shape
    n = w.shape[-1]
    k2, n2 = side.shape[1:]
    tm = m // PANEL_ROW_TILES
    steps = (n // tn) * PANEL_ROW_TILES
    n_slabs = min(steps, k2 // LANES)
    slab = k2 // n_slabs
    assert m % (PANEL_ROW_TILES * BF16_ROWS) == 0 and n % tn == 0 and tn % MXU_COLS == 0
    assert k2 % n_slabs == 0 and slab % BF16_ROWS == 0
    blocks = (_nbytes((tm, k), BF16) + _nbytes((k, tn), F32) + _nbytes((tm, tn), BF16)
              + _nbytes((slab, n2), F32) + _nbytes((slab, n2), BF16))
    temps = _nbytes((k, tn), BF16) + 3 * _nbytes((tm, MXU_COLS), F32)
    slab_index = lambda j, i: jnp.minimum(j * PANEL_ROW_TILES + i, n_slabs - 1)
    if panel_major:
        out_spec = pl.BlockSpec((None, tm, tn), lambda j, i: (j, i, 0))
        out_shape = jax.ShapeDtypeStruct((n // tn, m, tn), BF16)
    else:
        out_spec = pl.BlockSpec((tm, tn), lambda j, i: (i, j))
        out_shape = jax.ShapeDtypeStruct((m, n), BF16)
    return pl.pallas_call(
        functools.partial(_panel_matmul_kernel, act=act),
        grid=(n // tn, PANEL_ROW_TILES),
        in_specs=[
            pl.BlockSpec((tm, k), lambda j, i: (i, 0)),
            pl.BlockSpec((None, k, tn), lambda j, i: (w_layer, 0, j)),
            pl.BlockSpec((None, slab, n2), lambda j, i: (side_layer, slab_index(j, i), 0)),
        ],
        out_specs=[out_spec, pl.BlockSpec((slab, n2), lambda j, i: (slab_index(j, i), 0))],
        out_shape=[out_shape, jax.ShapeDtypeStruct((k2, n2), BF16)],
        scratch_shapes=[pltpu.VMEM((k, tn), BF16)],
        compiler_params=pltpu.CompilerParams(
            dimension_semantics=("arbitrary", "arbitrary"),
            vmem_limit_bytes=_vmem_limit(blocks, temps)),
        name=name,
    )(h, w, side)


def _kstream_matmul_kernel(a_ref, w_ref, xk_ref, gp_ref, gn_ref, o_ref, *rest, nk, row_blocks, emit_next):
    if emit_next:
        hn_ref, x_sc = rest
    else:
        (x_sc,) = rest
    k = pl.program_id(1)
    tm, n = o_ref.shape
    rb = tm // row_blocks
    for c in range(row_blocks):
        @pl.when(k == c)
        def _(c=c):
            x_sc[c * rb:(c + 1) * rb, :] = xk_ref[...]

    def accumulate(rows, assign):
        a = a_ref[rows, :]
        for c in range(0, n, 2 * MXU_COLS):
            cols = slice(c, c + 2 * MXU_COLS)
            part = jnp.dot(a, w_ref[:, cols], preferred_element_type=F32)
            if assign:
                o_ref[rows, cols] = part
            else:
                o_ref[rows, cols] += part

    @pl.when(k == 0)
    def _():
        accumulate(slice(None), True)

    @pl.when((k > 0) & (k < nk - 1))
    def _():
        accumulate(slice(None), False)

    @pl.when(k == nk - 1)
    def _():
        for r in range(0, tm, rb):
            rows = slice(r, r + rb)
            accumulate(rows, False)
            y = x_sc[rows, :] + _rms_scale(o_ref[rows, :], gp_ref[...])
            o_ref[rows, :] = y
            if emit_next:
                hn_ref[rows, :] = _rms_scale(y, gn_ref[...]).astype(BF16)


def _kstream_matmul(a, w, x, g_post, g_post_layer, g_next, g_next_layer, *, emit_next, name,
                    tm=None, n_tiles=KSTREAM_ROW_TILES, first_tile=0, row_blocks=EPILOGUE_ROW_BLOCKS):
    nk, m, tk = a.shape
    n = w.shape[-1]
    tm = m // n_tiles if tm is None else tm
    rb = tm // row_blocks
    assert (first_tile + n_tiles) * tm <= m and w.shape[0] == nk * tk and nk >= row_blocks
    assert tm % (row_blocks * BF16_ROWS) == 0 and n % (2 * MXU_COLS) == 0
    blocks = (_nbytes((tm, tk), BF16) + _nbytes((tk, n), BF16) + _nbytes((rb, n), F32)
              + _nbytes((tm, n), F32) + (_nbytes((tm, n), BF16) if emit_next else 0))
    temps = _nbytes((tm, n), F32) + 4 * _nbytes((rb, n), F32)
    out_specs = [pl.BlockSpec((tm, n), lambda i, k: (i, 0))]
    out_shape = [jax.ShapeDtypeStruct((n_tiles * tm, n), F32)]
    if emit_next:
        out_specs.append(pl.BlockSpec((tm, n), lambda i, k: (i, 0)))
        out_shape.append(jax.ShapeDtypeStruct((n_tiles * tm, n), BF16))
    return pl.pallas_call(
        functools.partial(_kstream_matmul_kernel, nk=nk, row_blocks=row_blocks, emit_next=emit_next),
        grid=(n_tiles, nk),
        in_specs=[
            pl.BlockSpec((None, tm, tk), lambda i, k: (k, first_tile + i, 0)),
            pl.BlockSpec((tk, n), lambda i, k: (k, 0)),
            pl.BlockSpec((rb, n), lambda i, k: ((first_tile + i) * row_blocks + jnp.minimum(k, row_blocks - 1), 0)),
            pl.BlockSpec((None, 1, n), lambda i, k: (g_post_layer, 0, 0)),
            pl.BlockSpec((None, 1, n), lambda i, k: (g_next_layer, 0, 0)),
        ],
        out_specs=out_specs,
        out_shape=out_shape,
        scratch_shapes=[pltpu.VMEM((tm, n), F32)],
        compiler_params=pltpu.CompilerParams(
            dimension_semantics=("arbitrary", "arbitrary"),
            vmem_limit_bytes=_vmem_limit(blocks, temps)),
        name=name,
    )(a, w, x, g_post, g_next)


def _project_rows(a_sc, rows, w_ref, o_ref):
    for c in range(0, o_ref.shape[1], 2 * MXU_COLS):
        cols = slice(c, c + 2 * MXU_COLS)
        o_ref[rows, cols] = jnp.dot(a_sc[rows, :], w_ref[:, cols], preferred_element_type=F32)


def _residual_block(x_refs, r, blk):
    if len(x_refs) == 1:
        return x_refs[0][r * blk:(r + 1) * blk, :]
    return _stream_block(x_refs, r)


def _finish_rows(x_rows, rows, gp_ref, gn_ref, o_ref, hn_ref):
    y = x_rows + _rms_scale(o_ref[rows, :], gp_ref[...])
    o_ref[rows, :] = y
    hn_ref[rows, :] = _rms_scale(y, gn_ref[...]).astype(BF16)


def _mix_project_call(kernel_fn, mix_inputs, mix_specs, tail, w, x, g_post, g_post_layer, g_next, g_next_layer, *,
                      blk, scratch_shapes, mix_bytes, name):
    kdim, n = w.shape
    tm = MIXER_BLOCKS * blk
    row_tile = lambda cols: pl.BlockSpec((tm, cols), lambda s: (s, 0))
    if isinstance(x, tuple):
        assert blk == TAIL_ROWS
        m = x[0].shape[0] + TAIL_ROWS
        x_inputs, x_specs = _stream_specs(*x)
    else:
        m = x.shape[0]
        x_inputs, x_specs = [x], [row_tile(n)]
    assert m % tm == 0 and n % (2 * MXU_COLS) == 0 and tail.shape == (blk, kdim)
    blocks = mix_bytes + _nbytes((blk, kdim), BF16) + 2 * _nbytes((tm, n), F32) + _nbytes((tm, n), BF16)
    temps = _nbytes((kdim, n), BF16) + _nbytes((tm, kdim), BF16) + 4 * _nbytes((blk, n), F32) + (4 << 20)
    return pl.pallas_call(
        functools.partial(kernel_fn, n_x=len(x_inputs)),
        grid=(m // tm,),
        in_specs=list(mix_specs) + [
            pl.BlockSpec((blk, kdim), lambda s: (0, 0)),
            pl.BlockSpec((kdim, n), lambda s: (0, 0), pipeline_mode=pl.Buffered(1)),
            *x_specs,
            pl.BlockSpec((None, 1, n), lambda s: (g_post_layer, 0, 0)),
            pl.BlockSpec((None, 1, n), lambda s: (g_next_layer, 0, 0)),
        ],
        out_specs=[row_tile(n), row_tile(n)],
        out_shape=[jax.ShapeDtypeStruct((m, n), F32), jax.ShapeDtypeStruct((m, n), BF16)],
        scratch_shapes=[pltpu.VMEM((tm, kdim), BF16)] + list(scratch_shapes),
        compiler_params=pltpu.CompilerParams(
            dimension_semantics=("arbitrary",), vmem_limit_bytes=_vmem_limit(blocks, temps)),
        name=name,
    )(*mix_inputs, tail, w, *x_inputs, g_post, g_next)


def _rope_tables(positions):
    half = HEAD_DIM // 2
    inv_freq = ROPE_THETA ** (-jnp.arange(half, dtype=F32) / half)
    ang = positions.astype(F32)[:, None] * inv_freq[None, :]
    cos = jnp.cos(ang)
    sin = jnp.sin(ang)
    reps = LANES // HEAD_DIM
    cos_t = jnp.tile(jnp.concatenate([cos, cos], axis=1), (1, reps))
    sin_t = jnp.tile(jnp.concatenate([-sin, sin], axis=1), (1, reps))
    return cos_t, sin_t


def _rope(x, cos, sin):
    half = HEAD_DIM // 2
    w = x.shape[1]
    lane = lax.broadcasted_iota(jnp.int32, x.shape, 1)
    first_half = (lane % HEAD_DIM) < half
    rot = jnp.where(first_half, pltpu.roll(x, w - half, axis=1), pltpu.roll(x, half, axis=1))
    return x * cos + rot * sin


def _rotated_keys(p_ref, cos, sin, *, n_heads, n_kv):
    q_w = n_heads * HEAD_DIM
    kv_w = n_kv * HEAD_DIM
    return _rope(p_ref[:, q_w:q_w + kv_w].astype(F32),
                 jnp.tile(cos, (1, kv_w // LANES)), jnp.tile(sin, (1, kv_w // LANES)))


def _own_keys(blk):
    return lax.broadcasted_iota(jnp.int32, (blk, blk), 0) <= lax.broadcasted_iota(jnp.int32, (blk, blk), 1)


def _swa_scores(r, n, p_ref, cos, sin, kslots_ref, vtslots_ref, *, n_heads, n_kv):
    blk = p_ref.shape[0]
    q_w = n_heads * HEAD_DIM
    kv_w = n_kv * HEAD_DIM
    gqa = n_heads // n_kv
    own_slot = slice((r + 1) * blk, (r + 2) * blk)
    band = slice(r * blk, (r + 2) * blk)
    kslots_ref[own_slot, :] = _rotated_keys(p_ref, cos, sin, n_heads=n_heads, n_kv=n_kv).astype(BF16)
    vtslots_ref[:, own_slot] = p_ref[:, q_w + kv_w:q_w + 2 * kv_w].astype(F32).T.astype(BF16)
    q_scale = HEAD_DIM ** -0.5 * LOG2_E
    cos_q = cos * q_scale
    sin_q = sin * q_scale
    k_heads = [kslots_ref[band, kv * HEAD_DIM:(kv + 1) * HEAD_DIM] for kv in range(n_kv)]
    heads_per_group = LANES // HEAD_DIM
    own = _own_keys(blk)
    no_prev = jnp.where(own | (n > 0), 0.0, -jnp.inf)
    scores = []
    for pair in range(q_w // LANES):
        q_rot = _rope(p_ref[:, pair * LANES:(pair + 1) * LANES].astype(F32), cos_q, sin_q).astype(BF16)
        for sub in range(heads_per_group):
            kv = (pair * heads_per_group + sub) // gqa
            q_h = q_rot[:, sub * HEAD_DIM:(sub + 1) * HEAD_DIM]
            s_band = lax.dot_general(k_heads[kv], q_h, (((1,), (1,)), ((), ())), preferred_element_type=F32)
            scores.append(jnp.where(own, s_band[blk:], s_band[:blk]) + no_prev)
    return scores


def _conv_gates(p_ref, *, n_heads, n_kv, conv_dim):
    g_off = (n_heads + 2 * n_kv) * HEAD_DIM
    return [p_ref[:, g_off + i * conv_dim:g_off + (i + 1) * conv_dim].astype(F32) for i in range(CONV_WIDTH)]


def _swa_mix(r, n, scores, z_prev, p_ref, sink_ref, cw_ref, out_ref, vtslots_ref, *, n_heads, n_kv):
    blk = p_ref.shape[0]
    q_w = n_heads * HEAD_DIM
    conv_dim = cw_ref.shape[1]
    gqa = n_heads // n_kv
    heads_per_group = LANES // HEAD_DIM
    rows = slice(r * blk, (r + 1) * blk)
    band = slice(r * blk, (r + 2) * blk)
    own = _own_keys(blk)
    p_bands, inv_denoms = [], []
    for h, s in enumerate(scores):
        sink = sink_ref[h] * LOG2_E
        m = jnp.maximum(jnp.max(s, axis=0, keepdims=True), sink)
        p = jnp.exp2(s - m)
        denom = jnp.sum(p, axis=0, keepdims=True) + jnp.exp2(sink - m)
        p_bands.append(jnp.concatenate([jnp.where(own, 0.0, p), jnp.where(own, p, 0.0)], axis=0).astype(BF16))
        inv_denoms.append(1.0 / denom)
    for pair in range(q_w // LANES):
        outs_t = []
        for sub in range(heads_per_group):
            h = pair * heads_per_group + sub
            kv = h // gqa
            o_t = jnp.dot(vtslots_ref[kv * HEAD_DIM:(kv + 1) * HEAD_DIM, band], p_bands[h],
                          preferred_element_type=F32)
            outs_t.append(o_t * inv_denoms[h])
        out_ref[rows, pair * LANES:(pair + 1) * LANES] = jnp.concatenate(outs_t, axis=0).T.astype(BF16)

    gate_b, gate_c, h_conv = _conv_gates(p_ref, n_heads=n_heads, n_kv=n_kv, conv_dim=conv_dim)
    z = gate_c * h_conv
    z_prev = jnp.where(n > 0, z_prev, 0.0)
    top = lax.broadcasted_iota(jnp.int32, z_prev.shape, 0)

    def delayed(d):
        rolled = pltpu.roll(z, d, axis=0)
        head = jnp.where(top < d, pltpu.roll(z_prev, d, axis=0), rolled[:SUBLANES])
        return jnp.concatenate([head, rolled[SUBLANES:]], axis=0)

    conv = cw_ref[0:1, :] * delayed(2)
    conv = conv + cw_ref[1:2, :] * delayed(1)
    conv = conv + cw_ref[2:3, :] * z
    out_ref[rows, q_w:] = (gate_b * conv).astype(BF16)
    return z[blk - SUBLANES:, :]


def _swa_out_kernel(p_ref, cos_ref, sin_ref, sink_ref, cw_ref, tail_ref, w_ref, *rest, n_heads, n_kv, blk, n_x):
    x_refs = rest[:n_x]
    gp_ref, gn_ref, o_ref, hn_ref, a_sc, kslots_ref, vtslots_ref, zprev_ref, znext_ref = rest[n_x:]
    nb = cos_ref.shape[0] // blk
    step = pl.program_id(0)
    tail_step = step == pl.num_programs(0) - 1
    heads = dict(n_heads=n_heads, n_kv=n_kv)
    first_slot = slice(0, blk)
    last_slot = slice(MIXER_BLOCKS * blk, (MIXER_BLOCKS + 1) * blk)

    @pl.when(step == 0)
    def _():
        kslots_ref[first_slot, :] = jnp.zeros((blk, kslots_ref.shape[1]), BF16)
        vtslots_ref[:, first_slot] = jnp.zeros((vtslots_ref.shape[0], blk), BF16)
        zprev_ref[...] = jnp.zeros_like(zprev_ref)

    @pl.when(step > 0)
    def _():
        kslots_ref[first_slot, :] = kslots_ref[last_slot, :]
        vtslots_ref[:, first_slot] = vtslots_ref[:, last_slot]
        zprev_ref[...] = znext_ref[...]

    def block_rows(r):
        return slice(r * blk, (r + 1) * blk)

    def scores(r):
        n = (step * MIXER_BLOCKS + r) % nb
        pos = pl.ds(pl.multiple_of(n * blk, blk), blk)
        return n, _swa_scores(r, n, p_ref.at[block_rows(r)], cos_ref[pos, :], sin_ref[pos, :], kslots_ref,
                              vtslots_ref, **heads)

    def mix(r, z_prev, n, s_bands):
        rows = block_rows(r)
        z_last = _swa_mix(r, n, s_bands, z_prev, p_ref.at[rows], sink_ref, cw_ref, a_sc, vtslots_ref, **heads)
        if r == MIXER_BLOCKS - 1:
            a_sc[rows, :] = jnp.where(tail_step, tail_ref[...], a_sc[rows, :])
        return z_last

    z_last = mix(0, zprev_ref[...], *scores(0))
    for r in range(MIXER_BLOCKS):
        if r + 1 < MIXER_BLOCKS:
            upcoming = scores(r + 1)
        _project_rows(a_sc, block_rows(r), w_ref, o_ref)
        if r + 1 < MIXER_BLOCKS:
            z_last = mix(r + 1, z_last, *upcoming)
        _finish_rows(_residual_block(x_refs, r, blk), block_rows(r), gp_ref, gn_ref, o_ref, hn_ref)
    znext_ref[...] = z_last


def _swa_out(p, cos, sin, sinks, conv_w, layer, tail, w, x, g_post, g_post_layer, g_next, g_next_layer, *,
             blk, n_heads, n_kv):
    pw = p.shape[1]
    seq = cos.shape[0]
    kv_w = n_kv * HEAD_DIM
    conv_dim = conv_w.shape[-1]
    tm = MIXER_BLOCKS * blk
    table = pl.BlockSpec((seq, LANES), lambda s: (0, 0), pipeline_mode=pl.Buffered(1))
    mix_specs = [
        pl.BlockSpec((tm, pw), lambda s: (s, 0)),
        table, table,
        pl.BlockSpec(memory_space=pltpu.SMEM),
        pl.BlockSpec((None, CONV_WIDTH, conv_dim), lambda s: (layer, 0, 0)),
    ]
    return _mix_project_call(
        functools.partial(_swa_out_kernel, n_heads=n_heads, n_kv=n_kv, blk=blk),
        (p, cos, sin, sinks, conv_w), mix_specs, tail, w, x, g_post, g_post_layer, g_next, g_next_layer,
        blk=blk,
        scratch_shapes=[pltpu.VMEM(((MIXER_BLOCKS + 1) * blk, kv_w), BF16),
                        pltpu.VMEM((kv_w, (MIXER_BLOCKS + 1) * blk), BF16),
                        pltpu.VMEM((SUBLANES, conv_dim), F32), pltpu.VMEM((SUBLANES, conv_dim), F32)],
        mix_bytes=_nbytes((tm, pw), BF16) + _nbytes((seq, LANES), F32),
        name="swa_out")


def _swa_last_rows_kernel(p_ref, cos_ref, sin_ref, kwin_ref, vwin_ref, ztail_ref, *, n_heads, n_kv):
    blk = p_ref.shape[0]
    q_w = n_heads * HEAD_DIM
    kv_w = n_kv * HEAD_DIM
    kwin_ref[...] = _rotated_keys(p_ref, cos_ref[...], sin_ref[...], n_heads=n_heads, n_kv=n_kv)
    vwin_ref[...] = p_ref[:, q_w + kv_w:q_w + 2 * kv_w].astype(F32)
    _, gate_c, h_conv = _conv_gates(p_ref, n_heads=n_heads, n_kv=n_kv, conv_dim=ztail_ref.shape[1])
    ztail_ref[...] = (gate_c * h_conv)[blk - SUBLANES:, :]


def _swa_last_rows(p, cos, sin, *, batch, blk, n_heads, n_kv, conv_dim):
    pw = p.shape[1]
    nb = cos.shape[0] // blk
    kv_w = n_kv * HEAD_DIM
    table = pl.BlockSpec((blk, LANES), lambda b: (nb - 1, 0))
    per_sequence = lambda b: (b, 0, 0)
    return pl.pallas_call(
        functools.partial(_swa_last_rows_kernel, n_heads=n_heads, n_kv=n_kv),
        grid=(batch,),
        in_specs=[pl.BlockSpec((blk, pw), lambda b: (b * nb + nb - 1, 0)), table, table],
        out_specs=[pl.BlockSpec((None, blk, kv_w), per_sequence), pl.BlockSpec((None, blk, kv_w), per_sequence),
                   pl.BlockSpec((None, SUBLANES, conv_dim), per_sequence)],
        out_shape=[jax.ShapeDtypeStruct((batch, blk, kv_w), F32), jax.ShapeDtypeStruct((batch, blk, kv_w), F32),
                   jax.ShapeDtypeStruct((batch, SUBLANES, conv_dim), F32)],
        compiler_params=pltpu.CompilerParams(dimension_semantics=("arbitrary",)),
        name="swa_last_rows",
    )(p, cos, sin)


def _swa_conv_step_kernel(q_ref, kv_ref, gates_ref, ck_ref, cv_ref, st_ref, cos_ref, sin_ref, sink_ref, cw_ref,
                          attn_ref, conv_ref, kout_ref, vout_ref, stout_ref, *, n_heads, n_kv):
    nb, window, kv_w = ck_ref.shape
    gqa = n_heads // n_kv
    row_h = lax.broadcasted_iota(jnp.int32, (n_heads, kv_w), 0)
    lane_h = lax.broadcasted_iota(jnp.int32, (n_heads, kv_w), 1)
    own = (lane_h // HEAD_DIM) == (row_h // gqa)
    key_pos = lax.broadcasted_iota(jnp.int32, (n_heads, window), 1)
    cos = cos_ref[...]
    sin = sin_ref[...]
    sink = sink_ref[...]
    scale = HEAD_DIM ** -0.5
    newest = lax.broadcasted_iota(jnp.int32, (window, kv_w), 0) == window - 1
    staged = []
    for i in range(nb):
        q = jnp.where(own, _rope(q_ref[i], cos, sin), 0.0)
        k_new = _rope(kv_ref[i, 0:1, :], cos, sin)
        s = lax.dot_general(q.astype(BF16), ck_ref[i].astype(BF16), (((1,), (1,)), ((), ())),
                            preferred_element_type=F32) * scale
        s_new = jnp.sum(q * k_new, axis=-1, keepdims=True) * scale
        kout_ref[i] = jnp.where(newest, k_new, pltpu.roll(ck_ref[i], window - 1, axis=0))
        staged.append((s, s_new))
    weights = []
    for s, s_new in staged:
        s = jnp.where(key_pos >= 1, s, -jnp.inf)
        m = jnp.maximum(jnp.maximum(jnp.max(s, axis=-1, keepdims=True), s_new), sink)
        p = jnp.exp(s - m)
        p_new = jnp.exp(s_new - m)
        denom = jnp.sum(p, axis=-1, keepdims=True) + p_new + jnp.exp(sink - m)
        weights.append((p.astype(BF16), p_new, denom))
    for i, (p, p_new, denom) in enumerate(weights):
        v_new = kv_ref[i, 1:2, :]
        o = jnp.dot(p, cv_ref[i].astype(BF16), preferred_element_type=F32)
        o = o + p_new * v_new
        o = jnp.where(own, o / denom, 0.0)
        folded = o[:, :LANES]
        for c in range(LANES, kv_w, LANES):
            folded = folded + o[:, c:c + LANES]
        for shift in range(HEAD_DIM, LANES, HEAD_DIM):
            folded = folded + pltpu.roll(folded, shift, axis=1)
        attn_ref[i] = folded
        vout_ref[i] = jnp.where(newest, v_new, pltpu.roll(cv_ref[i], window - 1, axis=0))

    for i in range(nb):
        gate_b = gates_ref[i, 0:1, :]
        z = gates_ref[i, 1:2, :] * gates_ref[i, 2:3, :]
        conv = cw_ref[0:1, :] * st_ref[i, 0:1, :]
        conv = conv + cw_ref[1:2, :] * st_ref[i, 1:2, :]
        conv = conv + cw_ref[2:3, :] * z
        conv_ref[i] = gate_b * conv
        stout_ref[i, 0:1, :] = st_ref[i, 1:2, :]
        stout_ref[i, 1:2, :] = z


def _swa_conv_step(q_rep, kv_new, gates, cache_k, cache_v, state, cos, sin, sinks, conv_w, layer, *,
                   nb, n_heads, n_kv):
    db, window, kv_w = cache_k.shape[1:]
    conv_dim = conv_w.shape[-1]
    seq = lambda i: (i, 0, 0)
    lay = lambda i: (layer, i, 0, 0)
    return pl.pallas_call(
        functools.partial(_swa_conv_step_kernel, n_heads=n_heads, n_kv=n_kv),
        grid=(db // nb,),
        in_specs=[
            pl.BlockSpec((nb, n_heads, kv_w), seq),
            pl.BlockSpec((nb, 2, kv_w), seq),
            pl.BlockSpec((nb, 3, conv_dim), seq),
            pl.BlockSpec((None, nb, window, kv_w), lay),
            pl.BlockSpec((None, nb, window, kv_w), lay),
            pl.BlockSpec((None, nb, CONV_WIDTH - 1, conv_dim), lay),
            pl.BlockSpec((1, kv_w), lambda i: (0, 0)),
            pl.BlockSpec((1, kv_w), lambda i: (0, 0)),
            pl.BlockSpec((None, n_heads, 1), lambda i: (layer, 0, 0)),
            pl.BlockSpec((None, CONV_WIDTH, conv_dim), lambda i: (layer, 0, 0)),
        ],
        out_specs=[
            pl.BlockSpec((nb, n_heads, LANES), seq),
            pl.BlockSpec((nb, 1, conv_dim), seq),
            pl.BlockSpec((nb, window, kv_w), seq),
            pl.BlockSpec((nb, window, kv_w), seq),
            pl.BlockSpec((nb, CONV_WIDTH - 1, conv_dim), seq),
        ],
        out_shape=[
            jax.ShapeDtypeStruct((db, n_heads, LANES), F32),
            jax.ShapeDtypeStruct((db, 1, conv_dim), F32),
            jax.ShapeDtypeStruct((db, window, kv_w), F32),
            jax.ShapeDtypeStruct((db, window, kv_w), F32),
            jax.ShapeDtypeStruct((db, CONV_WIDTH - 1, conv_dim), F32),
        ],
        compiler_params=pltpu.CompilerParams(dimension_semantics=("arbitrary",)),
        name="swa_conv_step",
    )(q_rep, kv_new, gates, cache_k, cache_v, state, cos, sin, sinks, conv_w)


def _layernorm(v, g, b):
    vc = v - jnp.mean(v, axis=-1, keepdims=True)
    var = jnp.mean(vc * vc, axis=-1, keepdims=True)
    return vc * lax.rsqrt(var + EPS) * g + b


def _cmlp_out_kernel(z_ref, lg_ref, lb_ref, ws_ref, bs_ref, tail_ref, w_ref, *rest, n_x):
    x_refs = rest[:n_x]
    gp_ref, gn_ref, o_ref, hn_ref, a_sc, wt_ref = rest[n_x:]
    groups, chunk = ws_ref.shape[:2]
    width = a_sc.shape[1]
    dg = width // groups
    tail_step = pl.program_id(0) == pl.num_programs(0) - 1

    @pl.when(pl.program_id(0) == 0)
    def _():
        row = lax.broadcasted_iota(jnp.int32, (chunk, chunk), 0)
        col = lax.broadcasted_iota(jnp.int32, (chunk, chunk), 1)
        for g in range(groups):
            wt_ref[g] = jnp.where(row >= col, ws_ref[g], 0.0).astype(BF16)

    def block_rows(r):
        return slice(r * chunk, (r + 1) * chunk)

    def mix(r):
        rows = block_rows(r)
        vn_b = _layernorm(z_ref[rows, width:].astype(F32), lg_ref[...], lb_ref[...]).astype(BF16)
        for g in range(groups):
            lanes = slice(g * dg, (g + 1) * dg)
            mixed = jnp.dot(wt_ref[g], vn_b[:, lanes], preferred_element_type=F32) + bs_ref[g]
            a_sc[rows, lanes] = (z_ref[rows, lanes].astype(F32) * mixed).astype(BF16)
        if r == MIXER_BLOCKS - 1:
            a_sc[rows, :] = jnp.where(tail_step, tail_ref[...], a_sc[rows, :])

    mix(0)
    for r in range(MIXER_BLOCKS):
        if r + 1 < MIXER_BLOCKS:
            mix(r + 1)
        _project_rows(a_sc, block_rows(r), w_ref, o_ref)
        _finish_rows(_residual_block(x_refs, r, chunk), block_rows(r), gp_ref, gn_ref, o_ref, hn_ref)


def _cmlp_out(z, ln_g, ln_b, w_s, b_s, layer, tail, w, x, g_post, g_post_layer, g_next, g_next_layer):
    zw = z.shape[1]
    width = zw // 2
    groups, chunk = w_s.shape[1:3]
    tm = MIXER_BLOCKS * chunk
    mix_specs = [
        pl.BlockSpec((tm, zw), lambda s: (s, 0)),
        pl.BlockSpec((None, 1, width), lambda s: (layer, 0, 0)),
        pl.BlockSpec((None, 1, width), lambda s: (layer, 0, 0)),
        pl.BlockSpec((None, groups, chunk, chunk), lambda s: (layer, 0, 0, 0)),
        pl.BlockSpec((None, groups, chunk, 1), lambda s: (layer, 0, 0, 0)),
    ]
    return _mix_project_call(
        _cmlp_out_kernel, (z, ln_g, ln_b, w_s, b_s), mix_specs, tail, w, x,
        g_post, g_post_layer, g_next, g_next_layer,
        blk=chunk, scratch_shapes=[pltpu.VMEM((groups, chunk, chunk), BF16)],
        mix_bytes=_nbytes((tm, zw), BF16) + 2 * _nbytes((groups, chunk, chunk), F32),
        name="cmlp_out")


def _cmlp_last_v_kernel(z_ref, lg_ref, lb_ref, v_ref):
    v_ref[...] = _layernorm(z_ref[:, v_ref.shape[1]:].astype(F32), lg_ref[...], lb_ref[...])


def _cmlp_last_v(z, ln_g, ln_b, layer, *, batch, n_chunks, chunk):
    zw = z.shape[1]
    width = zw // 2
    per_layer = pl.BlockSpec((None, 1, width), lambda b: (layer, 0, 0))
    return pl.pallas_call(
        _cmlp_last_v_kernel,
        grid=(batch,),
        in_specs=[pl.BlockSpec((chunk, zw), lambda b: (b * n_chunks + n_chunks - 1, 0)), per_layer, per_layer],
        out_specs=pl.BlockSpec((None, chunk, width), lambda b: (b, 0, 0)),
        out_shape=jax.ShapeDtypeStruct((batch, chunk, width), F32),
        compiler_params=pltpu.CompilerParams(dimension_semantics=("arbitrary",)),
        name="cmlp_last_v",
    )(z, ln_g, ln_b)


def _cmlp_gate_step_kernel(z_ref, lg_ref, lb_ref, scale_ref, bias_ref, o_ref, v_ref):
    width = o_ref.shape[1]
    vn = _layernorm(z_ref[:, width:], lg_ref[...], lb_ref[...])
    v_ref[...] = vn
    mix = scale_ref[...] * vn + bias_ref[...]
    o_ref[...] = (z_ref[:, :width] * mix).astype(BF16)


def _cmlp_gate_step(z, ln_g, ln_b, scale, bias, layer):
    rows, zw = z.shape
    width = zw // 2
    per_layer = pl.BlockSpec((None, 1, width), lambda i: (layer, 0, 0))
    return pl.pallas_call(
        _cmlp_gate_step_kernel,
        grid=(1,),
        in_specs=[pl.BlockSpec((rows, zw), lambda i: (0, 0)), per_layer, per_layer, per_layer, per_layer],
        out_specs=[pl.BlockSpec((rows, width), lambda i: (0, 0)), pl.BlockSpec((rows, width), lambda i: (0, 0))],
        out_shape=[jax.ShapeDtypeStruct((rows, width), BF16), jax.ShapeDtypeStruct((rows, width), F32)],
        compiler_params=pltpu.CompilerParams(dimension_semantics=("arbitrary",)),
        name="cmlp_gate_step",
    )(z, ln_g, ln_b, scale, bias)


def _tail_block(rows):
    return jnp.pad(rows.astype(BF16), ((0, TAIL_ROWS - rows.shape[0]), (0, 0)))


def kernel(x_prompt, x_sample, cache_k_win, cache_v_win, state_conv, w_in_even, w_out_even, conv_w, attn_sinks,
           w_in_cmlp, w_out_cmlp, ln_v_g, ln_v_b, w_spatial, b_spatial, w_ffn_up, w_ffn_down,
           g_mix_pre, g_mix_post, g_ffn_pre, g_ffn_post):
    batch, seq, d_model = x_prompt.shape
    dec_batch, dec_seq, _ = x_sample.shape
    assert dec_seq == 1, "the decode kernels take one new token per sequence"
    depth = g_mix_pre.shape[0]
    n_even, _, window, n_kv, head_dim = cache_k_win.shape
    assert head_dim == HEAD_DIM
    n_heads = attn_sinks.shape[1]
    q_w = n_heads * HEAD_DIM
    kv_w = n_kv * HEAD_DIM
    conv_dim = conv_w.shape[-1]
    n_odd, groups, chunk, _ = w_spatial.shape
    cmlp_w = w_out_cmlp.shape[1]
    n_prompt = batch * seq
    assert dec_batch <= TAIL_ROWS

    gains = [g.reshape(depth, 1, d_model) for g in (g_mix_pre, g_mix_post, g_ffn_pre, g_ffn_post)]
    g_mix_pre3, g_mix_post3, g_ffn_pre3, g_ffn_post3 = gains
    ln_g3 = ln_v_g.reshape(n_odd, 1, cmlp_w)
    ln_b3 = ln_v_b.reshape(n_odd, 1, cmlp_w)
    b_s4 = b_spatial.reshape(n_odd, groups, chunk, 1)
    step_scale = jnp.repeat(w_spatial[:, :, 0, 0], cmlp_w // groups, axis=1).reshape(n_odd, 1, cmlp_w)
    step_bias = jnp.repeat(b_spatial[:, :, 0], cmlp_w // groups, axis=1).reshape(n_odd, 1, cmlp_w)
    sinks3 = attn_sinks.reshape(n_even, n_heads, 1)
    cos_p, sin_p = _rope_tables(jnp.arange(seq))
    cos_s, sin_s = _rope_tables(PAST_LEN + jnp.arange(dec_seq))
    cos_s = jnp.tile(cos_s, (1, kv_w // LANES))
    sin_s = jnp.tile(sin_s, (1, kv_w // LANES))
    cache_k = cache_k_win.reshape(n_even, dec_batch, window, kv_w)
    cache_v = cache_v_win.reshape(n_even, dec_batch, window, kv_w)

    x = (x_prompt.reshape(n_prompt, d_model), x_sample.reshape(dec_batch, d_model))
    h = _stream_rmsnorm(*x, g_mix_pre3, 0)

    kp, vp, cp, up = [], [], [], []
    ks, vs, cs, us = [], [], [], []
    for i in range(depth):
        j = i // 2
        if i % 2 == 0:
            proj, w_out_b = _panel_matmul(h, w_in_even, j, w_out_even, j, act=_identity, tn=768, name="even_in")
            k_p, v_p, z_p = _swa_last_rows(proj, cos_p, sin_p, batch=batch, blk=window, n_heads=n_heads, n_kv=n_kv,
                                           conv_dim=conv_dim)
            kp.append(k_p.reshape(batch, window, n_kv, HEAD_DIM))
            vp.append(v_p.reshape(batch, window, n_kv, HEAD_DIM))
            cp.append(z_p[:, SUBLANES - (CONV_WIDTH - 1):, :])

            proj_s = proj[n_prompt:n_prompt + dec_batch].astype(F32)
            q_rep = jnp.tile(proj_s[:, :q_w].reshape(dec_batch, n_heads, HEAD_DIM), (1, 1, n_kv))
            kv_new = proj_s[:, q_w:q_w + 2 * kv_w].reshape(dec_batch, 2, kv_w)
            gates = proj_s[:, q_w + 2 * kv_w:].reshape(dec_batch, 3, conv_dim)
            attn_s, conv_s, k_s, v_s, c_s = _swa_conv_step(
                q_rep, kv_new, gates, cache_k, cache_v, state_conv, cos_s, sin_s, sinks3, conv_w, j,
                nb=8, n_heads=n_heads, n_kv=n_kv)
            cat_s = jnp.concatenate([attn_s[:, :, :HEAD_DIM].reshape(dec_batch, q_w),
                                     conv_s.reshape(dec_batch, conv_dim)], axis=1)
            ks.append(k_s.reshape(dec_batch, window, n_kv, HEAD_DIM))
            vs.append(v_s.reshape(dec_batch, window, n_kv, HEAD_DIM))
            cs.append(c_s)
            x, h = _swa_out(proj, cos_p, sin_p, attn_sinks[j], conv_w, j, _tail_block(cat_s), w_out_b, x,
                            g_mix_post3, i, g_ffn_pre3, i, blk=window, n_heads=n_heads, n_kv=n_kv)
        else:
            z, w_out_b = _panel_matmul(h, w_in_cmlp, j, w_out_cmlp, j, act=_gelu_exact, tn=1024, name="cmlp_in")
            up.append(_cmlp_last_v(z, ln_g3, ln_b3, j, batch=batch, n_chunks=seq // chunk, chunk=chunk))
            gated_s, v_s = _cmlp_gate_step(z[n_prompt:n_prompt + dec_batch].astype(F32), ln_g3, ln_b3,
                                           step_scale, step_bias, j)
            us.append(v_s.reshape(dec_batch, dec_seq, cmlp_w))
            x, h = _cmlp_out(z, ln_g3, ln_b3, w_spatial, b_s4, j, _tail_block(gated_s), w_out_b, x,
                             g_mix_post3, i, g_ffn_pre3, i)
        hidden, w_down_b = _panel_matmul(h, w_ffn_up, i, w_ffn_down, i, act=_relu_sq, tn=1024, name="ffn_up",
                                         panel_major=True)
        if i + 1 < depth:
            x, h = _kstream_matmul(hidden, w_down_b, x, g_ffn_post3, i, g_mix_pre3, i + 1,
                                   emit_next=True, name="ffn_down")
    last = depth - 1
    prompt_tile = n_prompt // KSTREAM_ROW_TILES
    (y_prompt,) = _kstream_matmul(hidden, w_down_b, x, g_ffn_post3, last, g_ffn_post3, last, emit_next=False,
                                  name="ffn_down_prompt", tm=prompt_tile, row_blocks=4)
    (y_tail,) = _kstream_matmul(hidden, w_down_b, x, g_ffn_post3, last, g_ffn_post3, last, emit_next=False,
                                name="ffn_down_tail", tm=TAIL_ROWS, n_tiles=1, first_tile=n_prompt // TAIL_ROWS,
                                row_blocks=1)

    return (y_prompt.reshape(batch, seq, d_model), y_tail[:dec_batch].reshape(dec_batch, dec_seq, d_model),
            jnp.stack(kp), jnp.stack(vp), jnp.stack(cp), jnp.stack(up),
            jnp.stack(ks), jnp.stack(vs), jnp.stack(cs), jnp.stack(us))
```

```python
import functools
import math

import jax
import jax.numpy as jnp
from jax import lax
from jax.experimental import pallas as pl
from jax.experimental.pallas import tpu as pltpu

F32 = jnp.float32
BF16 = jnp.bfloat16

EPS = 1e-6
HEAD_DIM = 64
ROPE_THETA = 10000.0
PAST_LEN = 16384
CONV_WIDTH = 3

LANES = 128
SUBLANES = 8
BF16_ROWS = 16
MXU_COLS = 256
VMEM_BYTES_V7X = 64 * 1024 * 1024

TAIL_ROWS = 128
PANEL_ROW_TILES = 4
KSTREAM_ROW_TILES = 8
FFN_DOWN_ROW_TILES = 10
EPILOGUE_ROW_BLOCKS = 5
MIXER_BLOCKS = 5
LOG2_E = math.log2(math.e)


def _vmem_limit(block_bytes, temp_bytes):
    need = 2 * block_bytes + temp_bytes + (4 << 20)
    return int(min(need, VMEM_BYTES_V7X - (6 << 20)))


def _nbytes(shape, dtype):
    return math.prod(shape) * jnp.dtype(dtype).itemsize


def _rms_scale(x, g):
    ms = jnp.mean(x * x, axis=-1, keepdims=True)
    return x * lax.rsqrt(ms + EPS) * g


def _identity(y):
    return y


def _gelu_exact(y):
    return 0.5 * y * (1.0 + lax.erf(y * math.sqrt(0.5)))


def _relu_sq(y):
    r = jnp.maximum(y, 0.0)
    return r * r


def _stream_specs(xp, xs):
    n_prompt, d = xp.shape
    n_blocks = n_prompt // TAIL_ROWS
    assert n_prompt % TAIL_ROWS == 0 and (n_blocks + 1) % MIXER_BLOCKS == 0
    assert xs.shape[0] % SUBLANES == 0 and xs.shape[0] <= TAIL_ROWS
    prompt_block = lambda r: pl.BlockSpec(
        (None, TAIL_ROWS, d), lambda s: (jnp.minimum(s * MIXER_BLOCKS + r, n_blocks - 1), 0, 0))
    specs = [prompt_block(r) for r in range(MIXER_BLOCKS)] + [pl.BlockSpec(xs.shape, lambda s: (0, 0))]
    return [xp.reshape(n_blocks, TAIL_ROWS, d)] * MIXER_BLOCKS + [xs], specs


def _stream_block(stream_refs, r):
    x = stream_refs[r][...]
    if r == MIXER_BLOCKS - 1:
        xs = stream_refs[MIXER_BLOCKS][...]
        tail = jnp.concatenate([xs, jnp.zeros((TAIL_ROWS - xs.shape[0], xs.shape[1]), xs.dtype)], axis=0)
        x = jnp.where(pl.program_id(0) == pl.num_programs(0) - 1, tail, x)
    return x


def _stream_rmsnorm_kernel(*refs):
    g_ref, h_ref = refs[MIXER_BLOCKS + 1:]
    for r in range(MIXER_BLOCKS):
        h_ref[r * TAIL_ROWS:(r + 1) * TAIL_ROWS, :] = _rms_scale(_stream_block(refs, r), g_ref[...]).astype(BF16)


def _stream_rmsnorm(xp, xs, g, g_layer):
    n_prompt, d = xp.shape
    rows = n_prompt + TAIL_ROWS
    tm = MIXER_BLOCKS * TAIL_ROWS
    inputs, specs = _stream_specs(xp, xs)
    blocks = _nbytes((tm, d), F32) + _nbytes((tm, d), BF16)
    return pl.pallas_call(
        _stream_rmsnorm_kernel,
        grid=(rows // tm,),
        in_specs=specs + [pl.BlockSpec((None, 1, d), lambda s: (g_layer, 0, 0))],
        out_specs=pl.BlockSpec((tm, d), lambda s: (s, 0)),
        out_shape=jax.ShapeDtypeStruct((rows, d), BF16),
        compiler_params=pltpu.CompilerParams(
            dimension_semantics=("arbitrary",), vmem_limit_bytes=_vmem_limit(blocks, 4 << 20)),
        name="stream_rmsnorm",
    )(*inputs, g)


def _panel_matmul_kernel(h_ref, w_ref, side_ref, o_ref, side_o_ref, wb_ref, *, act):
    def step(cast_panel):
        side_o_ref[...] = side_ref[...].astype(BF16)
        for c in range(0, o_ref.shape[1], MXU_COLS):
            cols = slice(c, c + MXU_COLS)
            if cast_panel:
                wb_ref[:, cols] = w_ref[:, cols].astype(BF16)
            y = jnp.dot(h_ref[...], wb_ref[:, cols], preferred_element_type=F32)
            o_ref[:, cols] = act(y).astype(o_ref.dtype)

    @pl.when(pl.program_id(1) == 0)
    def _():
        step(True)

    @pl.when(pl.program_id(1) > 0)
    def _():
        step(False)


def _panel_matmul(h, w, w_layer, side, side_layer, *, act, tn, name, panel_major=False):
    m, k = h.shape
    n = w.shape[-1]
    k2, n2 = side.shape[1:]
    tm = m // PANEL_ROW_TILES
    steps = (n // tn) * PANEL_ROW_TILES
    n_slabs = min(steps, k2 // LANES)
    slab = k2 // n_slabs
    assert m % (PANEL_ROW_TILES * BF16_ROWS) == 0 and n % tn == 0 and tn % MXU_COLS == 0
    assert k2 % n_slabs == 0 and slab % BF16_ROWS == 0
    blocks = (_nbytes((tm, k), BF16) + _nbytes((k, tn), F32) + _nbytes((tm, tn), BF16)
              + _nbytes((slab, n2), F32) + _nbytes((slab, n2), BF16))
    temps = _nbytes((k, tn), BF16) + 3 * _nbytes((tm, MXU_COLS), F32)
    slab_index = lambda j, i: jnp.minimum(j * PANEL_ROW_TILES + i, n_slabs - 1)
    if panel_major:
        out_spec = pl.BlockSpec((None, tm, tn), lambda j, i: (j, i, 0))
        out_shape = jax.ShapeDtypeStruct((n // tn, m, tn), BF16)
    else:
        out_spec = pl.BlockSpec((tm, tn), lambda j, i: (i, j))
        out_shape = jax.ShapeDtypeStruct((m, n), BF16)
    return pl.pallas_call(
        functools.partial(_panel_matmul_kernel, act=act),
        grid=(n // tn, PANEL_ROW_TILES),
        in_specs=[
            pl.BlockSpec((tm, k), lambda j, i: (i, 0)),
            pl.BlockSpec((None, k, tn), lambda j, i: (w_layer, 0, j)),
            pl.BlockSpec((None, slab, n2), lambda j, i: (side_layer, slab_index(j, i), 0)),
        ],
        out_specs=[out_spec, pl.BlockSpec((slab, n2), lambda j, i: (slab_index(j, i), 0))],
        out_shape=[out_shape, jax.ShapeDtypeStruct((k2, n2), BF16)],
        scratch_shapes=[pltpu.VMEM((k, tn), BF16)],
        compiler_params=pltpu.CompilerParams(
            dimension_semantics=("arbitrary", "arbitrary"),
            vmem_limit_bytes=_vmem_limit(blocks, temps)),
        name=name,
    )(h, w, side)


def _kstream_matmul_kernel(a_ref, w_ref, xk_ref, gp_ref, gn_ref, o_ref, *rest, nk, row_blocks, emit_next):
    if emit_next:
        hn_ref, x_sc = rest
    else:
        (x_sc,) = rest
    k = pl.program_id(1)
    tm, n = o_ref.shape
    rb = tm // row_blocks
    for c in range(row_blocks):
        @pl.when(k == c)
        def _(c=c):
            x_sc[c * rb:(c + 1) * rb, :] = xk_ref[...]

    def accumulate(rows, assign):
        panels, _, tk = a_ref.shape
        for c in range(0, n, 2 * MXU_COLS):
            cols = slice(c, c + 2 * MXU_COLS)
            part = jnp.dot(a_ref[0, rows, :], w_ref[:tk, cols], preferred_element_type=F32)
            for p in range(1, panels):
                part += jnp.dot(a_ref[p, rows, :], w_ref[p * tk:(p + 1) * tk, cols], preferred_element_type=F32)
            if assign:
                o_ref[rows, cols] = part
            else:
                o_ref[rows, cols] += part

    @pl.when(k == 0)
    def _():
        accumulate(slice(None), True)

    @pl.when((k > 0) & (k < nk - 1))
    def _():
        accumulate(slice(None), False)

    @pl.when(k == nk - 1)
    def _():
        for r in range(0, tm, rb):
            rows = slice(r, r + rb)
            accumulate(rows, False)
            y = x_sc[rows, :] + _rms_scale(o_ref[rows, :], gp_ref[...])
            o_ref[rows, :] = y
            if emit_next:
                hn_ref[rows, :] = _rms_scale(y, gn_ref[...]).astype(BF16)


def _kstream_matmul(a, w, x, g_post, g_post_layer, g_next, g_next_layer, *, emit_next, name,
                    tm=None, n_tiles=KSTREAM_ROW_TILES, first_tile=0, row_blocks=EPILOGUE_ROW_BLOCKS, k_panels=1):
    panels, m, tk = a.shape
    n = w.shape[-1]
    nk = panels // k_panels
    tm = m // n_tiles if tm is None else tm
    rb = tm // row_blocks
    assert (first_tile + n_tiles) * tm <= m and w.shape[0] == panels * tk and panels % k_panels == 0
    assert nk >= row_blocks and nk >= 2 and tm % (row_blocks * BF16_ROWS) == 0 and n % (2 * MXU_COLS) == 0
    blocks = (k_panels * _nbytes((tm, tk), BF16) + k_panels * _nbytes((tk, n), BF16) + _nbytes((rb, n), F32)
              + _nbytes((tm, n), F32) + (_nbytes((tm, n), BF16) if emit_next else 0))
    temps = _nbytes((tm, n), F32) + 4 * _nbytes((rb, n), F32)
    out_specs = [pl.BlockSpec((tm, n), lambda i, k: (i, 0))]
    out_shape = [jax.ShapeDtypeStruct((n_tiles * tm, n), F32)]
    if emit_next:
        out_specs.append(pl.BlockSpec((tm, n), lambda i, k: (i, 0)))
        out_shape.append(jax.ShapeDtypeStruct((n_tiles * tm, n), BF16))
    return pl.pallas_call(
        functools.partial(_kstream_matmul_kernel, nk=nk, row_blocks=row_blocks, emit_next=emit_next),
        grid=(n_tiles, nk),
        in_specs=[
            pl.BlockSpec((k_panels, tm, tk), lambda i, k: (k, first_tile + i, 0)),
            pl.BlockSpec((k_panels * tk, n), lambda i, k: (k, 0)),
            pl.BlockSpec((rb, n), lambda i, k: ((first_tile + i) * row_blocks + jnp.minimum(k, row_blocks - 1), 0)),
            pl.BlockSpec((None, 1, n), lambda i, k: (g_post_layer, 0, 0)),
            pl.BlockSpec((None, 1, n), lambda i, k: (g_next_layer, 0, 0)),
        ],
        out_specs=out_specs,
        out_shape=out_shape,
        scratch_shapes=[pltpu.VMEM((tm, n), F32)],
        compiler_params=pltpu.CompilerParams(
            dimension_semantics=("arbitrary", "arbitrary"),
            vmem_limit_bytes=_vmem_limit(blocks, temps)),
        name=name,
    )(a, w, x, g_post, g_next)


def _project_rows(a_sc, rows, w_ref, o_ref):
    for c in range(0, o_ref.shape[1], 2 * MXU_COLS):
        cols = slice(c, c + 2 * MXU_COLS)
        o_ref[rows, cols] = jnp.dot(a_sc[rows, :], w_ref[:, cols], preferred_element_type=F32)


def _residual_block(x_refs, r, blk):
    if len(x_refs) == 1:
        return x_refs[0][r * blk:(r + 1) * blk, :]
    return _stream_block(x_refs, r)


def _finish_rows(x_rows, rows, gp_ref, gn_ref, o_ref, hn_ref):
    y = x_rows + _rms_scale(o_ref[rows, :], gp_ref[...])
    o_ref[rows, :] = y
    hn_ref[rows, :] = _rms_scale(y, gn_ref[...]).astype(BF16)


def _mix_project_call(kernel_fn, mix_inputs, mix_specs, tail, w, x, g_post, g_post_layer, g_next, g_next_layer, *,
                      blk, scratch_shapes, mix_bytes, name):
    kdim, n = w.shape
    tm = MIXER_BLOCKS * blk
    row_tile = lambda cols: pl.BlockSpec((tm, cols), lambda s: (s, 0))
    if isinstance(x, tuple):
        assert blk == TAIL_ROWS
        m = x[0].shape[0] + TAIL_ROWS
        x_inputs, x_specs = _stream_specs(*x)
    else:
        m = x.shape[0]
        x_inputs, x_specs = [x], [row_tile(n)]
    assert m % tm == 0 and n % (2 * MXU_COLS) == 0 and tail.shape == (blk, kdim)
    blocks = mix_bytes + _nbytes((blk, kdim), BF16) + 2 * _nbytes((tm, n), F32) + _nbytes((tm, n), BF16)
    temps = _nbytes((kdim, n), BF16) + _nbytes((tm, kdim), BF16) + 4 * _nbytes((blk, n), F32) + (4 << 20)
    return pl.pallas_call(
        functools.partial(kernel_fn, n_x=len(x_inputs)),
        grid=(m // tm,),
        in_specs=list(mix_specs) + [
            pl.BlockSpec((blk, kdim), lambda s: (0, 0)),
            pl.BlockSpec((kdim, n), lambda s: (0, 0), pipeline_mode=pl.Buffered(1)),
            *x_specs,
            pl.BlockSpec((None, 1, n), lambda s: (g_post_layer, 0, 0)),
            pl.BlockSpec((None, 1, n), lambda s: (g_next_layer, 0, 0)),
        ],
        out_specs=[row_tile(n), row_tile(n)],
        out_shape=[jax.ShapeDtypeStruct((m, n), F32), jax.ShapeDtypeStruct((m, n), BF16)],
        scratch_shapes=[pltpu.VMEM((tm, kdim), BF16)] + list(scratch_shapes),
        compiler_params=pltpu.CompilerParams(
            dimension_semantics=("arbitrary",), vmem_limit_bytes=_vmem_limit(blocks, temps)),
        name=name,
    )(*mix_inputs, tail, w, *x_inputs, g_post, g_next)


def _rope_tables(positions):
    half = HEAD_DIM // 2
    inv_freq = ROPE_THETA ** (-jnp.arange(half, dtype=F32) / half)
    ang = positions.astype(F32)[:, None] * inv_freq[None, :]
    cos = jnp.cos(ang)
    sin = jnp.sin(ang)
    reps = LANES // HEAD_DIM
    cos_t = jnp.tile(jnp.concatenate([cos, cos], axis=1), (1, reps))
    sin_t = jnp.tile(jnp.concatenate([-sin, sin], axis=1), (1, reps))
    return cos_t, sin_t


def _rope(x, cos, sin):
    half = HEAD_DIM // 2
    w = x.shape[1]
    lane = lax.broadcasted_iota(jnp.int32, x.shape, 1)
    first_half = (lane % HEAD_DIM) < half
    rot = jnp.where(first_half, pltpu.roll(x, w - half, axis=1), pltpu.roll(x, half, axis=1))
    return x * cos + rot * sin


def _rotated_keys(p_ref, cos, sin, *, n_heads, n_kv):
    q_w = n_heads * HEAD_DIM
    kv_w = n_kv * HEAD_DIM
    return _rope(p_ref[:, q_w:q_w + kv_w].astype(F32),
                 jnp.tile(cos, (1, kv_w // LANES)), jnp.tile(sin, (1, kv_w // LANES)))


def _own_keys(blk):
    return lax.broadcasted_iota(jnp.int32, (blk, blk), 0) <= lax.broadcasted_iota(jnp.int32, (blk, blk), 1)


def _swa_scores(r, n, p_ref, cos, sin, kslots_ref, vtslots_ref, *, n_heads, n_kv):
    blk = p_ref.shape[0]
    q_w = n_heads * HEAD_DIM
    kv_w = n_kv * HEAD_DIM
    gqa = n_heads // n_kv
    own_slot = slice((r + 1) * blk, (r + 2) * blk)
    band = slice(r * blk, (r + 2) * blk)
    kslots_ref[own_slot, :] = _rotated_keys(p_ref, cos, sin, n_heads=n_heads, n_kv=n_kv).astype(BF16)
    vtslots_ref[:, own_slot] = p_ref[:, q_w + kv_w:q_w + 2 * kv_w].astype(F32).T.astype(BF16)
    q_scale = HEAD_DIM ** -0.5 * LOG2_E
    cos_q = cos * q_scale
    sin_q = sin * q_scale
    k_heads = [kslots_ref[band, kv * HEAD_DIM:(kv + 1) * HEAD_DIM] for kv in range(n_kv)]
    heads_per_group = LANES // HEAD_DIM
    own = _own_keys(blk)
    no_prev = jnp.where(own | (n > 0), 0.0, -jnp.inf)
    scores = []
    for pair in range(q_w // LANES):
        q_rot = _rope(p_ref[:, pair * LANES:(pair + 1) * LANES].astype(F32), cos_q, sin_q).astype(BF16)
        for sub in range(heads_per_group):
            kv = (pair * heads_per_group + sub) // gqa
            q_h = q_rot[:, sub * HEAD_DIM:(sub + 1) * HEAD_DIM]
            s_band = lax.dot_general(k_heads[kv], q_h, (((1,), (1,)), ((), ())), preferred_element_type=F32)
            scores.append(jnp.where(own, s_band[blk:], s_band[:blk]) + no_prev)
    return scores


def _conv_gates(p_ref, *, n_heads, n_kv, conv_dim):
    g_off = (n_heads + 2 * n_kv) * HEAD_DIM
    return [p_ref[:, g_off + i * conv_dim:g_off + (i + 1) * conv_dim].astype(F32) for i in range(CONV_WIDTH)]


def _swa_mix(r, n, scores, z_prev, p_ref, sink_ref, cw_ref, out_ref, vtslots_ref, *, n_heads, n_kv):
    blk = p_ref.shape[0]
    q_w = n_heads * HEAD_DIM
    conv_dim = cw_ref.shape[1]
    gqa = n_heads // n_kv
    heads_per_group = LANES // HEAD_DIM
    rows = slice(r * blk, (r + 1) * blk)
    band = slice(r * blk, (r + 2) * blk)
    own = _own_keys(blk)
    p_bands, inv_denoms = [], []
    for h, s in enumerate(scores):
        sink = sink_ref[h] * LOG2_E
        m = jnp.maximum(jnp.max(s, axis=0, keepdims=True), sink)
        p = jnp.exp2(s - m)
        denom = jnp.sum(p, axis=0, keepdims=True) + jnp.exp2(sink - m)
        p_bands.append(jnp.concatenate([jnp.where(own, 0.0, p), jnp.where(own, p, 0.0)], axis=0).astype(BF16))
        inv_denoms.append(1.0 / denom)
    for pair in range(q_w // LANES):
        outs_t = []
        for sub in range(heads_per_group):
            h = pair * heads_per_group + sub
            kv = h // gqa
            o_t = jnp.dot(vtslots_ref[kv * HEAD_DIM:(kv + 1) * HEAD_DIM, band], p_bands[h],
                          preferred_element_type=F32)
            outs_t.append(o_t * inv_denoms[h])
        out_ref[rows, pair * LANES:(pair + 1) * LANES] = jnp.concatenate(outs_t, axis=0).T.astype(BF16)

    gate_b, gate_c, h_conv = _conv_gates(p_ref, n_heads=n_heads, n_kv=n_kv, conv_dim=conv_dim)
    z = gate_c * h_conv
    z_prev = jnp.where(n > 0, z_prev, 0.0)
    top = lax.broadcasted_iota(jnp.int32, z_prev.shape, 0)

    def delayed(d):
        rolled = pltpu.roll(z, d, axis=0)
        head = jnp.where(top < d, pltpu.roll(z_prev, d, axis=0), rolled[:SUBLANES])
        return jnp.concatenate([head, rolled[SUBLANES:]], axis=0)

    conv = cw_ref[0:1, :] * delayed(2)
    conv = conv + cw_ref[1:2, :] * delayed(1)
    conv = conv + cw_ref[2:3, :] * z
    out_ref[rows, q_w:] = (gate_b * conv).astype(BF16)
    return z[blk - SUBLANES:, :]


def _swa_out_kernel(p_ref, cos_ref, sin_ref, sink_ref, cw_ref, tail_ref, w_ref, *rest, n_heads, n_kv, blk, n_x):
    x_refs = rest[:n_x]
    gp_ref, gn_ref, o_ref, hn_ref, a_sc, kslots_ref, vtslots_ref, zprev_ref, znext_ref = rest[n_x:]
    nb = cos_ref.shape[0] // blk
    step = pl.program_id(0)
    tail_step = step == pl.num_programs(0) - 1
    heads = dict(n_heads=n_heads, n_kv=n_kv)
    first_slot = slice(0, blk)
    last_slot = slice(MIXER_BLOCKS * blk, (MIXER_BLOCKS + 1) * blk)

    @pl.when(step == 0)
    def _():
        kslots_ref[first_slot, :] = jnp.zeros((blk, kslots_ref.shape[1]), BF16)
        vtslots_ref[:, first_slot] = jnp.zeros((vtslots_ref.shape[0], blk), BF16)
        zprev_ref[...] = jnp.zeros_like(zprev_ref)

    @pl.when(step > 0)
    def _():
        kslots_ref[first_slot, :] = kslots_ref[last_slot, :]
        vtslots_ref[:, first_slot] = vtslots_ref[:, last_slot]
        zprev_ref[...] = znext_ref[...]

    def block_rows(r):
        return slice(r * blk, (r + 1) * blk)

    def scores(r):
        n = (step * MIXER_BLOCKS + r) % nb
        pos = pl.ds(pl.multiple_of(n * blk, blk), blk)
        return n, _swa_scores(r, n, p_ref.at[block_rows(r)], cos_ref[pos, :], sin_ref[pos, :], kslots_ref,
                              vtslots_ref, **heads)

    def mix(r, z_prev, n, s_bands):
        rows = block_rows(r)
        z_last = _swa_mix(r, n, s_bands, z_prev, p_ref.at[rows], sink_ref, cw_ref, a_sc, vtslots_ref, **heads)
        if r == MIXER_BLOCKS - 1:
            a_sc[rows, :] = jnp.where(tail_step, tail_ref[...], a_sc[rows, :])
        return z_last

    z_last = mix(0, zprev_ref[...], *scores(0))
    for r in range(MIXER_BLOCKS):
        if r + 1 < MIXER_BLOCKS:
            upcoming = scores(r + 1)
        _project_rows(a_sc, block_rows(r), w_ref, o_ref)
        if r + 1 < MIXER_BLOCKS:
            z_last = mix(r + 1, z_last, *upcoming)
        _finish_rows(_residual_block(x_refs, r, blk), block_rows(r), gp_ref, gn_ref, o_ref, hn_ref)
    znext_ref[...] = z_last


def _swa_out(p, cos, sin, sinks, conv_w, layer, tail, w, x, g_post, g_post_layer, g_next, g_next_layer, *,
             blk, n_heads, n_kv):
    pw = p.shape[1]
    seq = cos.shape[0]
    kv_w = n_kv * HEAD_DIM
    conv_dim = conv_w.shape[-1]
    tm = MIXER_BLOCKS * blk
    table = pl.BlockSpec((seq, LANES), lambda s: (0, 0), pipeline_mode=pl.Buffered(1))
    mix_specs = [
        pl.BlockSpec((tm, pw), lambda s: (s, 0)),
        table, table,
        pl.BlockSpec(memory_space=pltpu.SMEM),
        pl.BlockSpec((None, CONV_WIDTH, conv_dim), lambda s: (layer, 0, 0)),
    ]
    return _mix_project_call(
        functools.partial(_swa_out_kernel, n_heads=n_heads, n_kv=n_kv, blk=blk),
        (p, cos, sin, sinks, conv_w), mix_specs, tail, w, x, g_post, g_post_layer, g_next, g_next_layer,
        blk=blk,
        scratch_shapes=[pltpu.VMEM(((MIXER_BLOCKS + 1) * blk, kv_w), BF16),
                        pltpu.VMEM((kv_w, (MIXER_BLOCKS + 1) * blk), BF16),
                        pltpu.VMEM((SUBLANES, conv_dim), F32), pltpu.VMEM((SUBLANES, conv_dim), F32)],
        mix_bytes=_nbytes((tm, pw), BF16) + _nbytes((seq, LANES), F32),
        name="swa_out")


def _swa_last_rows_kernel(p_ref, cos_ref, sin_ref, kwin_ref, vwin_ref, ztail_ref, *, n_heads, n_kv):
    blk = p_ref.shape[0]
    q_w = n_heads * HEAD_DIM
    kv_w = n_kv * HEAD_DIM
    kwin_ref[...] = _rotated_keys(p_ref, cos_ref[...], sin_ref[...], n_heads=n_heads, n_kv=n_kv)
    vwin_ref[...] = p_ref[:, q_w + kv_w:q_w + 2 * kv_w].astype(F32)
    _, gate_c, h_conv = _conv_gates(p_ref, n_heads=n_heads, n_kv=n_kv, conv_dim=ztail_ref.shape[1])
    ztail_ref[...] = (gate_c * h_conv)[blk - SUBLANES:, :]


def _swa_last_rows(p, cos, sin, *, batch, blk, n_heads, n_kv, conv_dim):
    pw = p.shape[1]
    nb = cos.shape[0] // blk
    kv_w = n_kv * HEAD_DIM
    table = pl.BlockSpec((blk, LANES), lambda b: (nb - 1, 0))
    per_sequence = lambda b: (b, 0, 0)
    return pl.pallas_call(
        functools.partial(_swa_last_rows_kernel, n_heads=n_heads, n_kv=n_kv),
        grid=(batch,),
        in_specs=[pl.BlockSpec((blk, pw), lambda b: (b * nb + nb - 1, 0)), table, table],
        out_specs=[pl.BlockSpec((None, blk, kv_w), per_sequence), pl.BlockSpec((None, blk, kv_w), per_sequence),
                   pl.BlockSpec((None, SUBLANES, conv_dim), per_sequence)],
        out_shape=[jax.ShapeDtypeStruct((batch, blk, kv_w), F32), jax.ShapeDtypeStruct((batch, blk, kv_w), F32),
                   jax.ShapeDtypeStruct((batch, SUBLANES, conv_dim), F32)],
        compiler_params=pltpu.CompilerParams(dimension_semantics=("arbitrary",)),
        name="swa_last_rows",
    )(p, cos, sin)


def _swa_conv_step_kernel(q_ref, kv_ref, gates_ref, ck_ref, cv_ref, st_ref, cos_ref, sin_ref, sink_ref, cw_ref,
                          attn_ref, conv_ref, kout_ref, vout_ref, stout_ref, *, n_heads, n_kv):
    nb, window, kv_w = ck_ref.shape
    gqa = n_heads // n_kv
    row_h = lax.broadcasted_iota(jnp.int32, (n_heads, kv_w), 0)
    lane_h = lax.broadcasted_iota(jnp.int32, (n_heads, kv_w), 1)
    own = (lane_h // HEAD_DIM) == (row_h // gqa)
    key_pos = lax.broadcasted_iota(jnp.int32, (n_heads, window), 1)
    cos = cos_ref[...]
    sin = sin_ref[...]
    sink = sink_ref[...]
    scale = HEAD_DIM ** -0.5
    newest = lax.broadcasted_iota(jnp.int32, (window, kv_w), 0) == window - 1
    staged = []
    for i in range(nb):
        q = jnp.where(own, _rope(q_ref[i], cos, sin), 0.0)
        k_new = _rope(kv_ref[i, 0:1, :], cos, sin)
        s = lax.dot_general(q.astype(BF16), ck_ref[i].astype(BF16), (((1,), (1,)), ((), ())),
                            preferred_element_type=F32) * scale
        s_new = jnp.sum(q * k_new, axis=-1, keepdims=True) * scale
        kout_ref[i] = jnp.where(newest, k_new, pltpu.roll(ck_ref[i], window - 1, axis=0))
        staged.append((s, s_new))
    weights = []
    for s, s_new in staged:
        s = jnp.where(key_pos >= 1, s, -jnp.inf)
        m = jnp.maximum(jnp.maximum(jnp.max(s, axis=-1, keepdims=True), s_new), sink)
        p = jnp.exp(s - m)
        p_new = jnp.exp(s_new - m)
        denom = jnp.sum(p, axis=-1, keepdims=True) + p_new + jnp.exp(sink - m)
        weights.append((p.astype(BF16), p_new, denom))
    for i, (p, p_new, denom) in enumerate(weights):
        v_new = kv_ref[i, 1:2, :]
        o = jnp.dot(p, cv_ref[i].astype(BF16), preferred_element_type=F32)
        o = o + p_new * v_new
        o = jnp.where(own, o / denom, 0.0)
        folded = o[:, :LANES]
        for c in range(LANES, kv_w, LANES):
            folded = folded + o[:, c:c + LANES]
        for shift in range(HEAD_DIM, LANES, HEAD_DIM):
            folded = folded + pltpu.roll(folded, shift, axis=1)
        attn_ref[i] = folded
        vout_ref[i] = jnp.where(newest, v_new, pltpu.roll(cv_ref[i], window - 1, axis=0))

    for i in range(nb):
        gate_b = gates_ref[i, 0:1, :]
        z = gates_ref[i, 1:2, :] * gates_ref[i, 2:3, :]
        conv = cw_ref[0:1, :] * st_ref[i, 0:1, :]
        conv = conv + cw_ref[1:2, :] * st_ref[i, 1:2, :]
        conv = conv + cw_ref[2:3, :] * z
        conv_ref[i] = gate_b * conv
        stout_ref[i, 0:1, :] = st_ref[i, 1:2, :]
        stout_ref[i, 1:2, :] = z


def _swa_conv_step(q_rep, kv_new, gates, cache_k, cache_v, state, cos, sin, sinks, conv_w, layer, *,
                   nb, n_heads, n_kv):
    db, window, kv_w = cache_k.shape[1:]
    conv_dim = conv_w.shape[-1]
    seq = lambda i: (i, 0, 0)
    lay = lambda i: (layer, i, 0, 0)
    return pl.pallas_call(
        functools.partial(_swa_conv_step_kernel, n_heads=n_heads, n_kv=n_kv),
        grid=(db // nb,),
        in_specs=[
            pl.BlockSpec((nb, n_heads, kv_w), seq),
            pl.BlockSpec((nb, 2, kv_w), seq),
            pl.BlockSpec((nb, 3, conv_dim), seq),
            pl.BlockSpec((None, nb, window, kv_w), lay),
            pl.BlockSpec((None, nb, window, kv_w), lay),
            pl.BlockSpec((None, nb, CONV_WIDTH - 1, conv_dim), lay),
            pl.BlockSpec((1, kv_w), lambda i: (0, 0)),
            pl.BlockSpec((1, kv_w), lambda i: (0, 0)),
            pl.BlockSpec((None, n_heads, 1), lambda i: (layer, 0, 0)),
            pl.BlockSpec((None, CONV_WIDTH, conv_dim), lambda i: (layer, 0, 0)),
        ],
        out_specs=[
            pl.BlockSpec((nb, n_heads, LANES), seq),
            pl.BlockSpec((nb, 1, conv_dim), seq),
            pl.BlockSpec((nb, window, kv_w), seq),
            pl.BlockSpec((nb, window, kv_w), seq),
            pl.BlockSpec((nb, CONV_WIDTH - 1, conv_dim), seq),
        ],
        out_shape=[
            jax.ShapeDtypeStruct((db, n_heads, LANES), F32),
            jax.ShapeDtypeStruct((db, 1, conv_dim), F32),
            jax.ShapeDtypeStruct((db, window, kv_w), F32),
            jax.ShapeDtypeStruct((db, window, kv_w), F32),
            jax.ShapeDtypeStruct((db, CONV_WIDTH - 1, conv_dim), F32),
        ],
        compiler_params=pltpu.CompilerParams(dimension_semantics=("arbitrary",)),
        name="swa_conv_step",
    )(q_rep, kv_new, gates, cache_k, cache_v, state, cos, sin, sinks, conv_w)


def _layernorm(v, g, b):
    vc = v - jnp.mean(v, axis=-1, keepdims=True)
    var = jnp.mean(vc * vc, axis=-1, keepdims=True)
    return vc * lax.rsqrt(var + EPS) * g + b


def _cmlp_out_kernel(z_ref, lg_ref, lb_ref, ws_ref, bs_ref, tail_ref, w_ref, *rest, n_x):
    x_refs = rest[:n_x]
    gp_ref, gn_ref, o_ref, hn_ref, a_sc, wt_ref = rest[n_x:]
    groups, chunk = ws_ref.shape[:2]
    width = a_sc.shape[1]
    dg = width // groups
    tail_step = pl.program_id(0) == pl.num_programs(0) - 1

    @pl.when(pl.program_id(0) == 0)
    def _():
        row = lax.broadcasted_iota(jnp.int32, (chunk, chunk), 0)
        col = lax.broadcasted_iota(jnp.int32, (chunk, chunk), 1)
        for g in range(groups):
            wt_ref[g] = jnp.where(row >= col, ws_ref[g], 0.0).astype(BF16)

    def block_rows(r):
        return slice(r * chunk, (r + 1) * chunk)

    def mix(r):
        rows = block_rows(r)
        vn_b = _layernorm(z_ref[rows, width:].astype(F32), lg_ref[...], lb_ref[...]).astype(BF16)
        for g in range(groups):
            lanes = slice(g * dg, (g + 1) * dg)
            mixed = jnp.dot(wt_ref[g], vn_b[:, lanes], preferred_element_type=F32) + bs_ref[g]
            a_sc[rows, lanes] = (z_ref[rows, lanes].astype(F32) * mixed).astype(BF16)
        if r == MIXER_BLOCKS - 1:
            a_sc[rows, :] = jnp.where(tail_step, tail_ref[...], a_sc[rows, :])

    mix(0)
    for r in range(MIXER_BLOCKS):
        if r + 1 < MIXER_BLOCKS:
            mix(r + 1)
        _project_rows(a_sc, block_rows(r), w_ref, o_ref)
        _finish_rows(_residual_block(x_refs, r, chunk), block_rows(r), gp_ref, gn_ref, o_ref, hn_ref)


def _cmlp_out(z, ln_g, ln_b, w_s, b_s, layer, tail, w, x, g_post, g_post_layer, g_next, g_next_layer):
    zw = z.shape[1]
    width = zw // 2
    groups, chunk = w_s.shape[1:3]
    tm = MIXER_BLOCKS * chunk
    mix_specs = [
        pl.BlockSpec((tm, zw), lambda s: (s, 0)),
        pl.BlockSpec((None, 1, width), lambda s: (layer, 0, 0)),
        pl.BlockSpec((None, 1, width), lambda s: (layer, 0, 0)),
        pl.BlockSpec((None, groups, chunk, chunk), lambda s: (layer, 0, 0, 0)),
        pl.BlockSpec((None, groups, chunk, 1), lambda s: (layer, 0, 0, 0)),
    ]
    return _mix_project_call(
        _cmlp_out_kernel, (z, ln_g, ln_b, w_s, b_s), mix_specs, tail, w, x,
        g_post, g_post_layer, g_next, g_next_layer,
        blk=chunk, scratch_shapes=[pltpu.VMEM((groups, chunk, chunk), BF16)],
        mix_bytes=_nbytes((tm, zw), BF16) + 2 * _nbytes((groups, chunk, chunk), F32),
        name="cmlp_out")


def _cmlp_last_v_kernel(z_ref, lg_ref, lb_ref, v_ref):
    v_ref[...] = _layernorm(z_ref[:, v_ref.shape[1]:].astype(F32), lg_ref[...], lb_ref[...])


def _cmlp_last_v(z, ln_g, ln_b, layer, *, batch, n_chunks, chunk):
    zw = z.shape[1]
    width = zw // 2
    per_layer = pl.BlockSpec((None, 1, width), lambda b: (layer, 0, 0))
    return pl.pallas_call(
        _cmlp_last_v_kernel,
        grid=(batch,),
        in_specs=[pl.BlockSpec((chunk, zw), lambda b: (b * n_chunks + n_chunks - 1, 0)), per_layer, per_layer],
        out_specs=pl.BlockSpec((None, chunk, width), lambda b: (b, 0, 0)),
        out_shape=jax.ShapeDtypeStruct((batch, chunk, width), F32),
        compiler_params=pltpu.CompilerParams(dimension_semantics=("arbitrary",)),
        name="cmlp_last_v",
    )(z, ln_g, ln_b)


def _cmlp_gate_step_kernel(z_ref, lg_ref, lb_ref, scale_ref, bias_ref, o_ref, v_ref):
    width = o_ref.shape[1]
    vn = _layernorm(z_ref[:, width:], lg_ref[...], lb_ref[...])
    v_ref[...] = vn
    mix = scale_ref[...] * vn + bias_ref[...]
    o_ref[...] = (z_ref[:, :width] * mix).astype(BF16)


def _cmlp_gate_step(z, ln_g, ln_b, scale, bias, layer):
    rows, zw = z.shape
    width = zw // 2
    per_layer = pl.BlockSpec((None, 1, width), lambda i: (layer, 0, 0))
    return pl.pallas_call(
        _cmlp_gate_step_kernel,
        grid=(1,),
        in_specs=[pl.BlockSpec((rows, zw), lambda i: (0, 0)), per_layer, per_layer, per_layer, per_layer],
        out_specs=[pl.BlockSpec((rows, width), lambda i: (0, 0)), pl.BlockSpec((rows, width), lambda i: (0, 0))],
        out_shape=[jax.ShapeDtypeStruct((rows, width), BF16), jax.ShapeDtypeStruct((rows, width), F32)],
        compiler_params=pltpu.CompilerParams(dimension_semantics=("arbitrary",)),
        name="cmlp_gate_step",
    )(z, ln_g, ln_b, scale, bias)


def _tail_block(rows):
    return jnp.pad(rows.astype(BF16), ((0, TAIL_ROWS - rows.shape[0]), (0, 0)))


def kernel(x_prompt, x_sample, cache_k_win, cache_v_win, state_conv, w_in_even, w_out_even, conv_w, attn_sinks,
           w_in_cmlp, w_out_cmlp, ln_v_g, ln_v_b, w_spatial, b_spatial, w_ffn_up, w_ffn_down,
           g_mix_pre, g_mix_post, g_ffn_pre, g_ffn_post):
    batch, seq, d_model = x_prompt.shape
    dec_batch, dec_seq, _ = x_sample.shape
    assert dec_seq == 1, "the decode kernels take one new token per sequence"
    depth = g_mix_pre.shape[0]
    n_even, _, window, n_kv, head_dim = cache_k_win.shape
    assert head_dim == HEAD_DIM
    n_heads = attn_sinks.shape[1]
    q_w = n_heads * HEAD_DIM
    kv_w = n_kv * HEAD_DIM
    conv_dim = conv_w.shape[-1]
    n_odd, groups, chunk, _ = w_spatial.shape
    cmlp_w = w_out_cmlp.shape[1]
    n_prompt = batch * seq
    assert dec_batch <= TAIL_ROWS

    gains = [g.reshape(depth, 1, d_model) for g in (g_mix_pre, g_mix_post, g_ffn_pre, g_ffn_post)]
    g_mix_pre3, g_mix_post3, g_ffn_pre3, g_ffn_post3 = gains
    ln_g3 = ln_v_g.reshape(n_odd, 1, cmlp_w)
    ln_b3 = ln_v_b.reshape(n_odd, 1, cmlp_w)
    b_s4 = b_spatial.reshape(n_odd, groups, chunk, 1)
    step_scale = jnp.repeat(w_spatial[:, :, 0, 0], cmlp_w // groups, axis=1).reshape(n_odd, 1, cmlp_w)
    step_bias = jnp.repeat(b_spatial[:, :, 0], cmlp_w // groups, axis=1).reshape(n_odd, 1, cmlp_w)
    sinks3 = attn_sinks.reshape(n_even, n_heads, 1)
    cos_p, sin_p = _rope_tables(jnp.arange(seq))
    cos_s, sin_s = _rope_tables(PAST_LEN + jnp.arange(dec_seq))
    cos_s = jnp.tile(cos_s, (1, kv_w // LANES))
    sin_s = jnp.tile(sin_s, (1, kv_w // LANES))
    cache_k = cache_k_win.reshape(n_even, dec_batch, window, kv_w)
    cache_v = cache_v_win.reshape(n_even, dec_batch, window, kv_w)

    x = (x_prompt.reshape(n_prompt, d_model), x_sample.reshape(dec_batch, d_model))
    h = _stream_rmsnorm(*x, g_mix_pre3, 0)

    kp, vp, cp, up = [], [], [], []
    ks, vs, cs, us = [], [], [], []
    for i in range(depth):
        j = i // 2
        if i % 2 == 0:
            proj, w_out_b = _panel_matmul(h, w_in_even, j, w_out_even, j, act=_identity, tn=768, name="even_in")
            k_p, v_p, z_p = _swa_last_rows(proj, cos_p, sin_p, batch=batch, blk=window, n_heads=n_heads, n_kv=n_kv,
                                           conv_dim=conv_dim)
            kp.append(k_p.reshape(batch, window, n_kv, HEAD_DIM))
            vp.append(v_p.reshape(batch, window, n_kv, HEAD_DIM))
            cp.append(z_p[:, SUBLANES - (CONV_WIDTH - 1):, :])

            proj_s = proj[n_prompt:n_prompt + dec_batch].astype(F32)
            q_rep = jnp.tile(proj_s[:, :q_w].reshape(dec_batch, n_heads, HEAD_DIM), (1, 1, n_kv))
            kv_new = proj_s[:, q_w:q_w + 2 * kv_w].reshape(dec_batch, 2, kv_w)
            gates = proj_s[:, q_w + 2 * kv_w:].reshape(dec_batch, 3, conv_dim)
            attn_s, conv_s, k_s, v_s, c_s = _swa_conv_step(
                q_rep, kv_new, gates, cache_k, cache_v, state_conv, cos_s, sin_s, sinks3, conv_w, j,
                nb=8, n_heads=n_heads, n_kv=n_kv)
            cat_s = jnp.concatenate([attn_s[:, :, :HEAD_DIM].reshape(dec_batch, q_w),
                                     conv_s.reshape(dec_batch, conv_dim)], axis=1)
            ks.append(k_s.reshape(dec_batch, window, n_kv, HEAD_DIM))
            vs.append(v_s.reshape(dec_batch, window, n_kv, HEAD_DIM))
            cs.append(c_s)
            x, h = _swa_out(proj, cos_p, sin_p, attn_sinks[j], conv_w, j, _tail_block(cat_s), w_out_b, x,
                            g_mix_post3, i, g_ffn_pre3, i, blk=window, n_heads=n_heads, n_kv=n_kv)
        else:
            z, w_out_b = _panel_matmul(h, w_in_cmlp, j, w_out_cmlp, j, act=_gelu_exact, tn=1024, name="cmlp_in")
            up.append(_cmlp_last_v(z, ln_g3, ln_b3, j, batch=batch, n_chunks=seq // chunk, chunk=chunk))
            gated_s, v_s = _cmlp_gate_step(z[n_prompt:n_prompt + dec_batch].astype(F32), ln_g3, ln_b3,
                                           step_scale, step_bias, j)
            us.append(v_s.reshape(dec_batch, dec_seq, cmlp_w))
            x, h = _cmlp_out(z, ln_g3, ln_b3, w_spatial, b_s4, j, _tail_block(gated_s), w_out_b, x,
                             g_mix_post3, i, g_ffn_pre3, i)
        hidden, w_down_b = _panel_matmul(h, w_ffn_up, i, w_ffn_down, i, act=_relu_sq, tn=1024, name="ffn_up",
                                         panel_major=True)
        if i + 1 < depth:
            x, h = _kstream_matmul(hidden, w_down_b, x, g_ffn_post3, i, g_mix_pre3, i + 1,
                                   emit_next=True, name="ffn_down", n_tiles=FFN_DOWN_ROW_TILES, row_blocks=4,
                                   k_panels=2)
    last = depth - 1
    prompt_tile = n_prompt // KSTREAM_ROW_TILES
    (y_prompt,) = _kstream_matmul(hidden, w_down_b, x, g_ffn_post3, last, g_ffn_post3, last, emit_next=False,
                                  name="ffn_down_prompt", tm=prompt_tile, row_blocks=4, k_panels=2)
    (y_tail,) = _kstream_matmul(hidden, w_down_b, x, g_ffn_post3, last, g_ffn_post3, last, emit_next=False,
                                name="ffn_down_tail", tm=TAIL_ROWS, n_tiles=1, first_tile=n_prompt // TAIL_ROWS,
                                row_blocks=1, k_panels=2)

    return (y_prompt.reshape(batch, seq, d_model), y_tail[:dec_batch].reshape(dec_batch, dec_seq, d_model),
            jnp.stack(kp), jnp.stack(vp), jnp.stack(cp), jnp.stack(up),
            jnp.stack(ks), jnp.stack(vs), jnp.stack(cs), jnp.stack(us))
```

```python
import functools
import math

import jax
import jax.numpy as jnp
from jax import lax
from jax.experimental import pallas as pl
from jax.experimental.pallas import tpu as pltpu

F32 = jnp.float32
BF16 = jnp.bfloat16

EPS = 1e-6
HEAD_DIM = 64
ROPE_THETA = 10000.0
PAST_LEN = 16384
CONV_WIDTH = 3

LANES = 128
SUBLANES = 8
BF16_ROWS = 16
MXU_COLS = 256
VMEM_BYTES_V7X = 64 * 1024 * 1024

TAIL_ROWS = 128
PANEL_ROW_TILES = 4
KSTREAM_ROW_TILES = 8
FFN_DOWN_ROW_TILES = 10
EPILOGUE_ROW_BLOCKS = 5
MIXER_BLOCKS = 5
LOG2_E = math.log2(math.e)


def _vmem_limit(block_bytes, temp_bytes):
    need = 2 * block_bytes + temp_bytes + (4 << 20)
    return int(min(need, VMEM_BYTES_V7X - (6 << 20)))


def _nbytes(shape, dtype):
    return math.prod(shape) * jnp.dtype(dtype).itemsize


def _rms_scale(x, g):
    ms = jnp.mean(x * x, axis=-1, keepdims=True)
    return x * lax.rsqrt(ms + EPS) * g


def _identity(y):
    return y


def _gelu_exact(y):
    return 0.5 * y * (1.0 + lax.erf(y * math.sqrt(0.5)))


def _relu_sq(y):
    r = jnp.maximum(y, 0.0)
    return r * r


def _stream_specs(xp, xs):
    n_prompt, d = xp.shape
    n_blocks = n_prompt // TAIL_ROWS
    assert n_prompt % TAIL_ROWS == 0 and (n_blocks + 1) % MIXER_BLOCKS == 0
    assert xs.shape[0] % SUBLANES == 0 and xs.shape[0] <= TAIL_ROWS
    prompt_block = lambda r: pl.BlockSpec(
        (None, TAIL_ROWS, d), lambda s: (jnp.minimum(s * MIXER_BLOCKS + r, n_blocks - 1), 0, 0))
    specs = [prompt_block(r) for r in range(MIXER_BLOCKS)] + [pl.BlockSpec(xs.shape, lambda s: (0, 0))]
    return [xp.reshape(n_blocks, TAIL_ROWS, d)] * MIXER_BLOCKS + [xs], specs


def _stream_block(stream_refs, r):
    x = stream_refs[r][...]
    if r == MIXER_BLOCKS - 1:
        xs = stream_refs[MIXER_BLOCKS][...]
        tail = jnp.concatenate([xs, jnp.zeros((TAIL_ROWS - xs.shape[0], xs.shape[1]), xs.dtype)], axis=0)
        x = jnp.where(pl.program_id(0) == pl.num_programs(0) - 1, tail, x)
    return x


def _stream_rmsnorm_kernel(*refs):
    g_ref, h_ref = refs[MIXER_BLOCKS + 1:]
    for r in range(MIXER_BLOCKS):
        h_ref[r * TAIL_ROWS:(r + 1) * TAIL_ROWS, :] = _rms_scale(_stream_block(refs, r), g_ref[...]).astype(BF16)


def _stream_rmsnorm(xp, xs, g, g_layer):
    n_prompt, d = xp.shape
    rows = n_prompt + TAIL_ROWS
    tm = MIXER_BLOCKS * TAIL_ROWS
    inputs, specs = _stream_specs(xp, xs)
    blocks = _nbytes((tm, d), F32) + _nbytes((tm, d), BF16)
    return pl.pallas_call(
        _stream_rmsnorm_kernel,
        grid=(rows // tm,),
        in_specs=specs + [pl.BlockSpec((None, 1, d), lambda s: (g_layer, 0, 0))],
        out_specs=pl.BlockSpec((tm, d), lambda s: (s, 0)),
        out_shape=jax.ShapeDtypeStruct((rows, d), BF16),
        compiler_params=pltpu.CompilerParams(
            dimension_semantics=("arbitrary",), vmem_limit_bytes=_vmem_limit(blocks, 4 << 20)),
        name="stream_rmsnorm",
    )(*inputs, g)


def _panel_matmul_kernel(h_ref, w_ref, side_ref, o_ref, side_o_ref, wb_ref, *, act):
    def step(cast_panel):
        side_o_ref[...] = side_ref[...].astype(BF16)
        for c in range(0, o_ref.shape[1], MXU_COLS):
            cols = slice(c, c + MXU_COLS)
            if cast_panel:
                wb_ref[:, cols] = w_ref[:, cols].astype(BF16)
            y = jnp.dot(h_ref[...], wb_ref[:, cols], preferred_element_type=F32)
            o_ref[:, cols] = act(y).astype(o_ref.dtype)

    @pl.when(pl.program_id(1) == 0)
    def _():
        step(True)

    @pl.when(pl.program_id(1) > 0)
    def _():
        step(False)


def _panel_matmul(h, w, w_layer, side, side_layer, *, act, tn, name, panel_major=False):
    m, k = h.shape
    n = w.shape[-1]
    k2, n2 = side.shape[1:]
    tm = m // PANEL_ROW_TILES
    steps = (n // tn) * PANEL_ROW_TILES
    n_slabs = min(steps, k2 // LANES)
    slab = k2 // n_slabs
    assert m % (PANEL_ROW_TILES * BF16_ROWS) == 0 and n % tn == 0 and tn % MXU_COLS == 0
    assert k2 % n_slabs == 0 and slab % BF16_ROWS == 0
    blocks = (_nbytes((tm, k), BF16) + _nbytes((k, tn), F32) + _nbytes((tm, tn), BF16)
              + _nbytes((slab, n2), F32) + _nbytes((slab, n2), BF16))
    temps = _nbytes((k, tn), BF16) + 3 * _nbytes((tm, MXU_COLS), F32)
    slab_index = lambda j, i: jnp.minimum(j * PANEL_ROW_TILES + i, n_slabs - 1)
    if panel_major:
        out_spec = pl.BlockSpec((None, tm, tn), lambda j, i: (j, i, 0))
        out_shape = jax.ShapeDtypeStruct((n // tn, m, tn), BF16)
    else:
        out_spec = pl.BlockSpec((tm, tn), lambda j, i: (i, j))
        out_shape = jax.ShapeDtypeStruct((m, n), BF16)
    return pl.pallas_call(
        functools.partial(_panel_matmul_kernel, act=act),
        grid=(n // tn, PANEL_ROW_TILES),
        in_specs=[
            pl.BlockSpec((tm, k), lambda j, i: (i, 0)),
            pl.BlockSpec((None, k, tn), lambda j, i: (w_layer, 0, j)),
            pl.BlockSpec((None, slab, n2), lambda j, i: (side_layer, slab_index(j, i), 0)),
        ],
        out_specs=[out_spec, pl.BlockSpec((slab, n2), lambda j, i: (slab_index(j, i), 0))],
        out_shape=[out_shape, jax.ShapeDtypeStruct((k2, n2), BF16)],
        scratch_shapes=[pltpu.VMEM((k, tn), BF16)],
        compiler_params=pltpu.CompilerParams(
            dimension_semantics=("arbitrary", "arbitrary"),
            vmem_limit_bytes=_vmem_limit(blocks, temps)),
        name=name,
    )(h, w, side)


def _kstream_matmul_kernel(a_ref, w_ref, xk_ref, gp_ref, gn_ref, o_ref, *rest, nk, row_blocks, emit_next):
    if emit_next:
        hn_ref, x_sc = rest
    else:
        (x_sc,) = rest
    k = pl.program_id(1)
    tm, n = o_ref.shape
    rb = tm // row_blocks
    for c in range(row_blocks):
        @pl.when(k == c)
        def _(c=c):
            x_sc[c * rb:(c + 1) * rb, :] = xk_ref[...]

    def accumulate(rows, assign):
        panels, _, tk = a_ref.shape
        for c in range(0, n, 2 * MXU_COLS):
            cols = slice(c, c + 2 * MXU_COLS)
            part = jnp.dot(a_ref[0, rows, :], w_ref[:tk, cols], preferred_element_type=F32)
            for p in range(1, panels):
                part += jnp.dot(a_ref[p, rows, :], w_ref[p * tk:(p + 1) * tk, cols], preferred_element_type=F32)
            if assign:
                o_ref[rows, cols] = part
            else:
                o_ref[rows, cols] += part

    @pl.when(k == 0)
    def _():
        accumulate(slice(None), True)

    @pl.when((k > 0) & (k < nk - 1))
    def _():
        accumulate(slice(None), False)

    @pl.when(k == nk - 1)
    def _():
        for r in range(0, tm, rb):
            rows = slice(r, r + rb)
            accumulate(rows, False)
            y = x_sc[rows, :] + _rms_scale(o_ref[rows, :], gp_ref[...])
            o_ref[rows, :] = y
            if emit_next:
                hn_ref[rows, :] = _rms_scale(y, gn_ref[...]).astype(BF16)


def _kstream_matmul(a, w, x, g_post, g_post_layer, g_next, g_next_layer, *, emit_next, name,
                    tm=None, n_tiles=KSTREAM_ROW_TILES, first_tile=0, row_blocks=EPILOGUE_ROW_BLOCKS, k_panels=1):
    panels, m, tk = a.shape
    n = w.shape[-1]
    nk = panels // k_panels
    tm = m // n_tiles if tm is None else tm
    rb = tm // row_blocks
    assert (first_tile + n_tiles) * tm <= m and w.shape[0] == panels * tk and panels % k_panels == 0
    assert nk >= row_blocks and nk >= 2 and tm % (row_blocks * BF16_ROWS) == 0 and n % (2 * MXU_COLS) == 0
    blocks = (k_panels * _nbytes((tm, tk), BF16) + k_panels * _nbytes((tk, n), BF16) + _nbytes((rb, n), F32)
              + _nbytes((tm, n), F32) + (_nbytes((tm, n), BF16) if emit_next else 0))
    temps = _nbytes((tm, n), F32) + 4 * _nbytes((rb, n), F32)
    out_specs = [pl.BlockSpec((tm, n), lambda i, k: (i, 0))]
    out_shape = [jax.ShapeDtypeStruct((n_tiles * tm, n), F32)]
    if emit_next:
        out_specs.append(pl.BlockSpec((tm, n), lambda i, k: (i, 0)))
        out_shape.append(jax.ShapeDtypeStruct((n_tiles * tm, n), BF16))
    return pl.pallas_call(
        functools.partial(_kstream_matmul_kernel, nk=nk, row_blocks=row_blocks, emit_next=emit_next),
        grid=(n_tiles, nk),
        in_specs=[
            pl.BlockSpec((k_panels, tm, tk), lambda i, k: (k, first_tile + i, 0)),
            pl.BlockSpec((k_panels * tk, n), lambda i, k: (k, 0)),
            pl.BlockSpec((rb, n), lambda i, k: ((first_tile + i) * row_blocks + jnp.minimum(k, row_blocks - 1), 0)),
            pl.BlockSpec((None, 1, n), lambda i, k: (g_post_layer, 0, 0)),
            pl.BlockSpec((None, 1, n), lambda i, k: (g_next_layer, 0, 0)),
        ],
        out_specs=out_specs,
        out_shape=out_shape,
        scratch_shapes=[pltpu.VMEM((tm, n), F32)],
        compiler_params=pltpu.CompilerParams(
            dimension_semantics=("arbitrary", "arbitrary"),
            vmem_limit_bytes=_vmem_limit(blocks, temps)),
        name=name,
    )(a, w, x, g_post, g_next)


def _project_rows(a_sc, rows, w_ref, o_ref):
    for c in range(0, o_ref.shape[1], 2 * MXU_COLS):
        cols = slice(c, c + 2 * MXU_COLS)
        o_ref[rows, cols] = jnp.dot(a_sc[rows, :], w_ref[:, cols], preferred_element_type=F32)


def _residual_block(x_refs, r, blk):
    if len(x_refs) == 1:
        return x_refs[0][r * blk:(r + 1) * blk, :]
    return _stream_block(x_refs, r)


def _finish_rows(x_rows, rows, gp_ref, gn_ref, o_ref, hn_ref):
    y = x_rows + _rms_scale(o_ref[rows, :], gp_ref[...])
    o_ref[rows, :] = y
    hn_ref[rows, :] = _rms_scale(y, gn_ref[...]).astype(BF16)


def _mix_project_call(kernel_fn, mix_inputs, mix_specs, w, x, g_post, g_post_layer, g_next, g_next_layer, *,
                      blk, scratch_shapes, mix_bytes, name, extra_out_specs=(), extra_out_shapes=()):
    kdim, n = w.shape
    tm = MIXER_BLOCKS * blk
    row_tile = lambda cols: pl.BlockSpec((tm, cols), lambda s: (s, 0))
    if isinstance(x, tuple):
        assert blk == TAIL_ROWS
        m = x[0].shape[0] + TAIL_ROWS
        x_inputs, x_specs = _stream_specs(*x)
    else:
        m = x.shape[0]
        x_inputs, x_specs = [x], [row_tile(n)]
    assert m % tm == 0 and n % (2 * MXU_COLS) == 0
    blocks = mix_bytes + 2 * _nbytes((tm, n), F32) + _nbytes((tm, n), BF16)
    temps = _nbytes((kdim, n), BF16) + _nbytes((tm, kdim), BF16) + 4 * _nbytes((blk, n), F32) + (4 << 20)
    return pl.pallas_call(
        functools.partial(kernel_fn, n_x=len(x_inputs)),
        grid=(m // tm,),
        in_specs=list(mix_specs) + [
            pl.BlockSpec((kdim, n), lambda s: (0, 0), pipeline_mode=pl.Buffered(1)),
            *x_specs,
            pl.BlockSpec((None, 1, n), lambda s: (g_post_layer, 0, 0)),
            pl.BlockSpec((None, 1, n), lambda s: (g_next_layer, 0, 0)),
        ],
        out_specs=[row_tile(n), row_tile(n), *extra_out_specs],
        out_shape=[jax.ShapeDtypeStruct((m, n), F32), jax.ShapeDtypeStruct((m, n), BF16), *extra_out_shapes],
        scratch_shapes=[pltpu.VMEM((tm, kdim), BF16)] + list(scratch_shapes),
        compiler_params=pltpu.CompilerParams(
            dimension_semantics=("arbitrary",), vmem_limit_bytes=_vmem_limit(blocks, temps)),
        name=name,
    )(*mix_inputs, w, *x_inputs, g_post, g_next)


def _rope_tables(positions):
    half = HEAD_DIM // 2
    inv_freq = ROPE_THETA ** (-jnp.arange(half, dtype=F32) / half)
    ang = positions.astype(F32)[:, None] * inv_freq[None, :]
    cos = jnp.cos(ang)
    sin = jnp.sin(ang)
    reps = LANES // HEAD_DIM
    cos_t = jnp.tile(jnp.concatenate([cos, cos], axis=1), (1, reps))
    sin_t = jnp.tile(jnp.concatenate([-sin, sin], axis=1), (1, reps))
    return cos_t, sin_t


def _rope(x, cos, sin):
    half = HEAD_DIM // 2
    w = x.shape[1]
    lane = lax.broadcasted_iota(jnp.int32, x.shape, 1)
    first_half = (lane % HEAD_DIM) < half
    rot = jnp.where(first_half, pltpu.roll(x, w - half, axis=1), pltpu.roll(x, half, axis=1))
    return x * cos + rot * sin


def _rotated_keys(p_ref, cos, sin, *, n_heads, n_kv):
    q_w = n_heads * HEAD_DIM
    kv_w = n_kv * HEAD_DIM
    return _rope(p_ref[:, q_w:q_w + kv_w].astype(F32),
                 jnp.tile(cos, (1, kv_w // LANES)), jnp.tile(sin, (1, kv_w // LANES)))


def _own_keys(blk):
    return lax.broadcasted_iota(jnp.int32, (blk, blk), 0) <= lax.broadcasted_iota(jnp.int32, (blk, blk), 1)


def _swa_scores(r, n, p_ref, cos, sin, kslots_ref, vtslots_ref, *, n_heads, n_kv):
    blk = p_ref.shape[0]
    q_w = n_heads * HEAD_DIM
    kv_w = n_kv * HEAD_DIM
    gqa = n_heads // n_kv
    own_slot = slice((r + 1) * blk, (r + 2) * blk)
    band = slice(r * blk, (r + 2) * blk)
    kslots_ref[own_slot, :] = _rotated_keys(p_ref, cos, sin, n_heads=n_heads, n_kv=n_kv).astype(BF16)
    vtslots_ref[:, own_slot] = p_ref[:, q_w + kv_w:q_w + 2 * kv_w].astype(F32).T.astype(BF16)
    q_scale = HEAD_DIM ** -0.5 * LOG2_E
    cos_q = cos * q_scale
    sin_q = sin * q_scale
    k_heads = [kslots_ref[band, kv * HEAD_DIM:(kv + 1) * HEAD_DIM] for kv in range(n_kv)]
    heads_per_group = LANES // HEAD_DIM
    own = _own_keys(blk)
    no_prev = jnp.where(own | (n > 0), 0.0, -jnp.inf)
    scores = []
    for pair in range(q_w // LANES):
        q_rot = _rope(p_ref[:, pair * LANES:(pair + 1) * LANES].astype(F32), cos_q, sin_q).astype(BF16)
        for sub in range(heads_per_group):
            kv = (pair * heads_per_group + sub) // gqa
            q_h = q_rot[:, sub * HEAD_DIM:(sub + 1) * HEAD_DIM]
            s_band = lax.dot_general(k_heads[kv], q_h, (((1,), (1,)), ((), ())), preferred_element_type=F32)
            scores.append(jnp.where(own, s_band[blk:], s_band[:blk]) + no_prev)
    return scores


def _conv_gates(p_ref, *, n_heads, n_kv, conv_dim):
    g_off = (n_heads + 2 * n_kv) * HEAD_DIM
    return [p_ref[:, g_off + i * conv_dim:g_off + (i + 1) * conv_dim].astype(F32) for i in range(CONV_WIDTH)]


def _swa_mix(r, n, scores, z_prev, p_ref, sink_ref, cw_ref, out_ref, vtslots_ref, *, n_heads, n_kv):
    blk = p_ref.shape[0]
    q_w = n_heads * HEAD_DIM
    conv_dim = cw_ref.shape[1]
    gqa = n_heads // n_kv
    heads_per_group = LANES // HEAD_DIM
    rows = slice(r * blk, (r + 1) * blk)
    band = slice(r * blk, (r + 2) * blk)
    own = _own_keys(blk)
    p_bands, inv_denoms = [], []
    for h, s in enumerate(scores):
        sink = sink_ref[h] * LOG2_E
        m = jnp.maximum(jnp.max(s, axis=0, keepdims=True), sink)
        p = jnp.exp2(s - m)
        denom = jnp.sum(p, axis=0, keepdims=True) + jnp.exp2(sink - m)
        p_bands.append(jnp.concatenate([jnp.where(own, 0.0, p), jnp.where(own, p, 0.0)], axis=0).astype(BF16))
        inv_denoms.append(1.0 / denom)
    for pair in range(q_w // LANES):
        outs_t = []
        for sub in range(heads_per_group):
            h = pair * heads_per_group + sub
            kv = h // gqa
            o_t = jnp.dot(vtslots_ref[kv * HEAD_DIM:(kv + 1) * HEAD_DIM, band], p_bands[h],
                          preferred_element_type=F32)
            outs_t.append(o_t * inv_denoms[h])
        out_ref[rows, pair * LANES:(pair + 1) * LANES] = jnp.concatenate(outs_t, axis=0).T.astype(BF16)

    gate_b, gate_c, h_conv = _conv_gates(p_ref, n_heads=n_heads, n_kv=n_kv, conv_dim=conv_dim)
    z = gate_c * h_conv
    z_prev = jnp.where(n > 0, z_prev, 0.0)
    top = lax.broadcasted_iota(jnp.int32, z_prev.shape, 0)

    def delayed(d):
        rolled = pltpu.roll(z, d, axis=0)
        head = jnp.where(top < d, pltpu.roll(z_prev, d, axis=0), rolled[:SUBLANES])
        return jnp.concatenate([head, rolled[SUBLANES:]], axis=0)

    conv = cw_ref[0:1, :] * delayed(2)
    conv = conv + cw_ref[1:2, :] * delayed(1)
    conv = conv + cw_ref[2:3, :] * z
    out_ref[rows, q_w:] = (gate_b * conv).astype(BF16)
    return z[blk - SUBLANES:, :]


def _swa_out_kernel(p_ref, cos_ref, sin_ref, sink_ref, cw_ref, tail_ref, w_ref, *rest, n_heads, n_kv, blk, n_x):
    x_refs = rest[:n_x]
    gp_ref, gn_ref, o_ref, hn_ref, a_sc, kslots_ref, vtslots_ref, zprev_ref, znext_ref = rest[n_x:]
    nb = cos_ref.shape[0] // blk
    step = pl.program_id(0)
    tail_step = step == pl.num_programs(0) - 1
    heads = dict(n_heads=n_heads, n_kv=n_kv)
    first_slot = slice(0, blk)
    last_slot = slice(MIXER_BLOCKS * blk, (MIXER_BLOCKS + 1) * blk)

    @pl.when(step == 0)
    def _():
        kslots_ref[first_slot, :] = jnp.zeros((blk, kslots_ref.shape[1]), BF16)
        vtslots_ref[:, first_slot] = jnp.zeros((vtslots_ref.shape[0], blk), BF16)
        zprev_ref[...] = jnp.zeros_like(zprev_ref)

    @pl.when(step > 0)
    def _():
        kslots_ref[first_slot, :] = kslots_ref[last_slot, :]
        vtslots_ref[:, first_slot] = vtslots_ref[:, last_slot]
        zprev_ref[...] = znext_ref[...]

    def block_rows(r):
        return slice(r * blk, (r + 1) * blk)

    def scores(r):
        n = (step * MIXER_BLOCKS + r) % nb
        pos = pl.ds(pl.multiple_of(n * blk, blk), blk)
        return n, _swa_scores(r, n, p_ref.at[block_rows(r)], cos_ref[pos, :], sin_ref[pos, :], kslots_ref,
                              vtslots_ref, **heads)

    def mix(r, z_prev, n, s_bands):
        rows = block_rows(r)
        z_last = _swa_mix(r, n, s_bands, z_prev, p_ref.at[rows], sink_ref, cw_ref, a_sc, vtslots_ref, **heads)
        if r == MIXER_BLOCKS - 1:
            a_sc[rows, :] = jnp.where(tail_step, tail_ref[...], a_sc[rows, :])
        return z_last

    z_last = mix(0, zprev_ref[...], *scores(0))
    for r in range(MIXER_BLOCKS):
        if r + 1 < MIXER_BLOCKS:
            upcoming = scores(r + 1)
        _project_rows(a_sc, block_rows(r), w_ref, o_ref)
        if r + 1 < MIXER_BLOCKS:
            z_last = mix(r + 1, z_last, *upcoming)
        _finish_rows(_residual_block(x_refs, r, blk), block_rows(r), gp_ref, gn_ref, o_ref, hn_ref)
    znext_ref[...] = z_last


def _swa_out(p, cos, sin, sinks, conv_w, layer, tail, w, x, g_post, g_post_layer, g_next, g_next_layer, *,
             blk, n_heads, n_kv):
    pw = p.shape[1]
    seq = cos.shape[0]
    kv_w = n_kv * HEAD_DIM
    conv_dim = conv_w.shape[-1]
    tm = MIXER_BLOCKS * blk
    table = pl.BlockSpec((seq, LANES), lambda s: (0, 0), pipeline_mode=pl.Buffered(1))
    mix_specs = [
        pl.BlockSpec((tm, pw), lambda s: (s, 0)),
        table, table,
        pl.BlockSpec(memory_space=pltpu.SMEM),
        pl.BlockSpec((None, CONV_WIDTH, conv_dim), lambda s: (layer, 0, 0)),
        pl.BlockSpec(tail.shape, lambda s: (0, 0)),
    ]
    assert tail.shape == (blk, w.shape[0])
    return _mix_project_call(
        functools.partial(_swa_out_kernel, n_heads=n_heads, n_kv=n_kv, blk=blk),
        (p, cos, sin, sinks, conv_w, tail), mix_specs, w, x, g_post, g_post_layer, g_next, g_next_layer,
        blk=blk,
        scratch_shapes=[pltpu.VMEM(((MIXER_BLOCKS + 1) * blk, kv_w), BF16),
                        pltpu.VMEM((kv_w, (MIXER_BLOCKS + 1) * blk), BF16),
                        pltpu.VMEM((SUBLANES, conv_dim), F32), pltpu.VMEM((SUBLANES, conv_dim), F32)],
        mix_bytes=_nbytes((tm, pw), BF16) + _nbytes((seq, LANES), F32),
        name="swa_out")


def _swa_last_rows_kernel(p_ref, cos_ref, sin_ref, kwin_ref, vwin_ref, ztail_ref, *, n_heads, n_kv):
    blk = p_ref.shape[0]
    q_w = n_heads * HEAD_DIM
    kv_w = n_kv * HEAD_DIM
    kwin_ref[...] = _rotated_keys(p_ref, cos_ref[...], sin_ref[...], n_heads=n_heads, n_kv=n_kv)
    vwin_ref[...] = p_ref[:, q_w + kv_w:q_w + 2 * kv_w].astype(F32)
    _, gate_c, h_conv = _conv_gates(p_ref, n_heads=n_heads, n_kv=n_kv, conv_dim=ztail_ref.shape[1])
    ztail_ref[...] = (gate_c * h_conv)[blk - SUBLANES:, :]


def _swa_last_rows(p, cos, sin, *, batch, blk, n_heads, n_kv, conv_dim):
    pw = p.shape[1]
    nb = cos.shape[0] // blk
    kv_w = n_kv * HEAD_DIM
    table = pl.BlockSpec((blk, LANES), lambda b: (nb - 1, 0))
    per_sequence = lambda b: (b, 0, 0)
    return pl.pallas_call(
        functools.partial(_swa_last_rows_kernel, n_heads=n_heads, n_kv=n_kv),
        grid=(batch,),
        in_specs=[pl.BlockSpec((blk, pw), lambda b: (b * nb + nb - 1, 0)), table, table],
        out_specs=[pl.BlockSpec((None, blk, kv_w), per_sequence), pl.BlockSpec((None, blk, kv_w), per_sequence),
                   pl.BlockSpec((None, SUBLANES, conv_dim), per_sequence)],
        out_shape=[jax.ShapeDtypeStruct((batch, blk, kv_w), F32), jax.ShapeDtypeStruct((batch, blk, kv_w), F32),
                   jax.ShapeDtypeStruct((batch, SUBLANES, conv_dim), F32)],
        compiler_params=pltpu.CompilerParams(dimension_semantics=("arbitrary",)),
        name="swa_last_rows",
    )(p, cos, sin)


def _swa_conv_step_kernel(q_ref, kv_ref, gates_ref, ck_ref, cv_ref, st_ref, cos_ref, sin_ref, sink_ref, cw_ref,
                          attn_ref, conv_ref, kout_ref, vout_ref, stout_ref, *, n_heads, n_kv):
    nb, window, kv_w = ck_ref.shape
    gqa = n_heads // n_kv
    row_h = lax.broadcasted_iota(jnp.int32, (n_heads, kv_w), 0)
    lane_h = lax.broadcasted_iota(jnp.int32, (n_heads, kv_w), 1)
    own = (lane_h // HEAD_DIM) == (row_h // gqa)
    key_pos = lax.broadcasted_iota(jnp.int32, (n_heads, window), 1)
    cos = cos_ref[...]
    sin = sin_ref[...]
    sink = sink_ref[...]
    scale = HEAD_DIM ** -0.5
    newest = lax.broadcasted_iota(jnp.int32, (window, kv_w), 0) == window - 1
    staged = []
    for i in range(nb):
        q = jnp.where(own, _rope(q_ref[i], cos, sin), 0.0)
        k_new = _rope(kv_ref[i, 0:1, :], cos, sin)
        s = lax.dot_general(q.astype(BF16), ck_ref[i].astype(BF16), (((1,), (1,)), ((), ())),
                            preferred_element_type=F32) * scale
        s_new = jnp.sum(q * k_new, axis=-1, keepdims=True) * scale
        kout_ref[i] = jnp.where(newest, k_new, pltpu.roll(ck_ref[i], window - 1, axis=0))
        staged.append((s, s_new))
    weights = []
    for s, s_new in staged:
        s = jnp.where(key_pos >= 1, s, -jnp.inf)
        m = jnp.maximum(jnp.maximum(jnp.max(s, axis=-1, keepdims=True), s_new), sink)
        p = jnp.exp(s - m)
        p_new = jnp.exp(s_new - m)
        denom = jnp.sum(p, axis=-1, keepdims=True) + p_new + jnp.exp(sink - m)
        weights.append((p.astype(BF16), p_new, denom))
    for i, (p, p_new, denom) in enumerate(weights):
        v_new = kv_ref[i, 1:2, :]
        o = jnp.dot(p, cv_ref[i].astype(BF16), preferred_element_type=F32)
        o = o + p_new * v_new
        o = jnp.where(own, o / denom, 0.0)
        folded = o[:, :LANES]
        for c in range(LANES, kv_w, LANES):
            folded = folded + o[:, c:c + LANES]
        for shift in range(HEAD_DIM, LANES, HEAD_DIM):
            folded = folded + pltpu.roll(folded, shift, axis=1)
        attn_ref[i] = folded
        vout_ref[i] = jnp.where(newest, v_new, pltpu.roll(cv_ref[i], window - 1, axis=0))

    for i in range(nb):
        gate_b = gates_ref[i, 0:1, :]
        z = gates_ref[i, 1:2, :] * gates_ref[i, 2:3, :]
        conv = cw_ref[0:1, :] * st_ref[i, 0:1, :]
        conv = conv + cw_ref[1:2, :] * st_ref[i, 1:2, :]
        conv = conv + cw_ref[2:3, :] * z
        conv_ref[i] = gate_b * conv
        stout_ref[i, 0:1, :] = st_ref[i, 1:2, :]
        stout_ref[i, 1:2, :] = z


def _swa_conv_step(q_rep, kv_new, gates, cache_k, cache_v, state, cos, sin, sinks, conv_w, layer, *,
                   nb, n_heads, n_kv):
    db, window, kv_w = cache_k.shape[1:]
    conv_dim = conv_w.shape[-1]
    seq = lambda i: (i, 0, 0)
    lay = lambda i: (layer, i, 0, 0)
    return pl.pallas_call(
        functools.partial(_swa_conv_step_kernel, n_heads=n_heads, n_kv=n_kv),
        grid=(db // nb,),
        in_specs=[
            pl.BlockSpec((nb, n_heads, kv_w), seq),
            pl.BlockSpec((nb, 2, kv_w), seq),
            pl.BlockSpec((nb, 3, conv_dim), seq),
            pl.BlockSpec((None, nb, window, kv_w), lay),
            pl.BlockSpec((None, nb, window, kv_w), lay),
            pl.BlockSpec((None, nb, CONV_WIDTH - 1, conv_dim), lay),
            pl.BlockSpec((1, kv_w), lambda i: (0, 0)),
            pl.BlockSpec((1, kv_w), lambda i: (0, 0)),
            pl.BlockSpec((None, n_heads, 1), lambda i: (layer, 0, 0)),
            pl.BlockSpec((None, CONV_WIDTH, conv_dim), lambda i: (layer, 0, 0)),
        ],
        out_specs=[
            pl.BlockSpec((nb, n_heads, LANES), seq),
            pl.BlockSpec((nb, 1, conv_dim), seq),
            pl.BlockSpec((nb, window, kv_w), seq),
            pl.BlockSpec((nb, window, kv_w), seq),
            pl.BlockSpec((nb, CONV_WIDTH - 1, conv_dim), seq),
        ],
        out_shape=[
            jax.ShapeDtypeStruct((db, n_heads, LANES), F32),
            jax.ShapeDtypeStruct((db, 1, conv_dim), F32),
            jax.ShapeDtypeStruct((db, window, kv_w), F32),
            jax.ShapeDtypeStruct((db, window, kv_w), F32),
            jax.ShapeDtypeStruct((db, CONV_WIDTH - 1, conv_dim), F32),
        ],
        compiler_params=pltpu.CompilerParams(dimension_semantics=("arbitrary",)),
        name="swa_conv_step",
    )(q_rep, kv_new, gates, cache_k, cache_v, state, cos, sin, sinks, conv_w)


def _layernorm(v, g, b):
    vc = v - jnp.mean(v, axis=-1, keepdims=True)
    var = jnp.mean(vc * vc, axis=-1, keepdims=True)
    return vc * lax.rsqrt(var + EPS) * g + b


def _cmlp_out_kernel(z_ref, lg_ref, lb_ref, ws_ref, bs_ref, scale_ref, bias_ref, w_ref, *rest, n_x):
    x_refs = rest[:n_x]
    gp_ref, gn_ref, o_ref, hn_ref, vtail_ref, a_sc, wt_ref = rest[n_x:]
    groups, chunk = ws_ref.shape[:2]
    width = a_sc.shape[1]
    dg = width // groups
    tail_step = pl.program_id(0) == pl.num_programs(0) - 1

    @pl.when(pl.program_id(0) == 0)
    def _():
        row = lax.broadcasted_iota(jnp.int32, (chunk, chunk), 0)
        col = lax.broadcasted_iota(jnp.int32, (chunk, chunk), 1)
        for g in range(groups):
            wt_ref[g] = jnp.where(row >= col, ws_ref[g], 0.0).astype(BF16)

    def block_rows(r):
        return slice(r * chunk, (r + 1) * chunk)

    def mix(r):
        rows = block_rows(r)
        vn = _layernorm(z_ref[rows, width:].astype(F32), lg_ref[...], lb_ref[...])
        vn_b = vn.astype(BF16)
        for g in range(groups):
            lanes = slice(g * dg, (g + 1) * dg)
            mixed = jnp.dot(wt_ref[g], vn_b[:, lanes], preferred_element_type=F32) + bs_ref[g]
            a_sc[rows, lanes] = (z_ref[rows, lanes].astype(F32) * mixed).astype(BF16)
        if r == MIXER_BLOCKS - 1:
            vtail_ref[...] = vn
            decode = z_ref[rows, :width].astype(F32) * (scale_ref[...] * vn + bias_ref[...])
            a_sc[rows, :] = jnp.where(tail_step, decode.astype(BF16), a_sc[rows, :])

    mix(0)
    for r in range(MIXER_BLOCKS):
        if r + 1 < MIXER_BLOCKS:
            mix(r + 1)
        _project_rows(a_sc, block_rows(r), w_ref, o_ref)
        _finish_rows(_residual_block(x_refs, r, chunk), block_rows(r), gp_ref, gn_ref, o_ref, hn_ref)


def _cmlp_out(z, ln_g, ln_b, w_s, b_s, step_scale, step_bias, layer, w, x, g_post, g_post_layer, g_next,
              g_next_layer):
    zw = z.shape[1]
    width = zw // 2
    groups, chunk = w_s.shape[1:3]
    tm = MIXER_BLOCKS * chunk
    assert chunk == TAIL_ROWS
    per_layer = pl.BlockSpec((None, 1, width), lambda s: (layer, 0, 0))
    mix_specs = [
        pl.BlockSpec((tm, zw), lambda s: (s, 0)),
        per_layer, per_layer,
        pl.BlockSpec((None, groups, chunk, chunk), lambda s: (layer, 0, 0, 0)),
        pl.BlockSpec((None, groups, chunk, 1), lambda s: (layer, 0, 0, 0)),
        per_layer, per_layer,
    ]
    return _mix_project_call(
        _cmlp_out_kernel, (z, ln_g, ln_b, w_s, b_s, step_scale, step_bias), mix_specs, w, x,
        g_post, g_post_layer, g_next, g_next_layer,
        blk=chunk, scratch_shapes=[pltpu.VMEM((groups, chunk, chunk), BF16)],
        mix_bytes=_nbytes((tm, zw), BF16) + 2 * _nbytes((groups, chunk, chunk), F32) + _nbytes((chunk, width), F32),
        extra_out_specs=[pl.BlockSpec((chunk, width), lambda s: (0, 0))],
        extra_out_shapes=[jax.ShapeDtypeStruct((chunk, width), F32)],
        name="cmlp_out")


def _cmlp_last_v_kernel(z_ref, lg_ref, lb_ref, v_ref):
    v_ref[...] = _layernorm(z_ref[:, v_ref.shape[1]:].astype(F32), lg_ref[...], lb_ref[...])


def _cmlp_last_v(z, ln_g, ln_b, layer, *, batch, n_chunks, chunk):
    zw = z.shape[1]
    width = zw // 2
    per_layer = pl.BlockSpec((None, 1, width), lambda b: (layer, 0, 0))
    return pl.pallas_call(
        _cmlp_last_v_kernel,
        grid=(batch,),
        in_specs=[pl.BlockSpec((chunk, zw), lambda b: (b * n_chunks + n_chunks - 1, 0)), per_layer, per_layer],
        out_specs=pl.BlockSpec((None, chunk, width), lambda b: (b, 0, 0)),
        out_shape=jax.ShapeDtypeStruct((batch, chunk, width), F32),
        compiler_params=pltpu.CompilerParams(dimension_semantics=("arbitrary",)),
        name="cmlp_last_v",
    )(z, ln_g, ln_b)


def _tail_block(rows):
    return jnp.pad(rows.astype(BF16), ((0, TAIL_ROWS - rows.shape[0]), (0, 0)))


def kernel(x_prompt, x_sample, cache_k_win, cache_v_win, state_conv, w_in_even, w_out_even, conv_w, attn_sinks,
           w_in_cmlp, w_out_cmlp, ln_v_g, ln_v_b, w_spatial, b_spatial, w_ffn_up, w_ffn_down,
           g_mix_pre, g_mix_post, g_ffn_pre, g_ffn_post):
    batch, seq, d_model = x_prompt.shape
    dec_batch, dec_seq, _ = x_sample.shape
    assert dec_seq == 1, "the decode kernels take one new token per sequence"
    depth = g_mix_pre.shape[0]
    n_even, _, window, n_kv, head_dim = cache_k_win.shape
    assert head_dim == HEAD_DIM
    n_heads = attn_sinks.shape[1]
    q_w = n_heads * HEAD_DIM
    kv_w = n_kv * HEAD_DIM
    conv_dim = conv_w.shape[-1]
    n_odd, groups, chunk, _ = w_spatial.shape
    cmlp_w = w_out_cmlp.shape[1]
    n_prompt = batch * seq
    assert dec_batch <= TAIL_ROWS

    gains = [g.reshape(depth, 1, d_model) for g in (g_mix_pre, g_mix_post, g_ffn_pre, g_ffn_post)]
    g_mix_pre3, g_mix_post3, g_ffn_pre3, g_ffn_post3 = gains
    ln_g3 = ln_v_g.reshape(n_odd, 1, cmlp_w)
    ln_b3 = ln_v_b.reshape(n_odd, 1, cmlp_w)
    b_s4 = b_spatial.reshape(n_odd, groups, chunk, 1)
    step_scale = jnp.repeat(w_spatial[:, :, 0, 0], cmlp_w // groups, axis=1).reshape(n_odd, 1, cmlp_w)
    step_bias = jnp.repeat(b_spatial[:, :, 0], cmlp_w // groups, axis=1).reshape(n_odd, 1, cmlp_w)
    sinks3 = attn_sinks.reshape(n_even, n_heads, 1)
    cos_p, sin_p = _rope_tables(jnp.arange(seq))
    cos_s, sin_s = _rope_tables(PAST_LEN + jnp.arange(dec_seq))
    cos_s = jnp.tile(cos_s, (1, kv_w // LANES))
    sin_s = jnp.tile(sin_s, (1, kv_w // LANES))
    cache_k = cache_k_win.reshape(n_even, dec_batch, window, kv_w)
    cache_v = cache_v_win.reshape(n_even, dec_batch, window, kv_w)

    x = (x_prompt.reshape(n_prompt, d_model), x_sample.reshape(dec_batch, d_model))
    h = _stream_rmsnorm(*x, g_mix_pre3, 0)

    kp, vp, cp, up = [], [], [], []
    ks, vs, cs, us = [], [], [], []
    for i in range(depth):
        j = i // 2
        if i % 2 == 0:
            proj, w_out_b = _panel_matmul(h, w_in_even, j, w_out_even, j, act=_identity, tn=768, name="even_in")
            k_p, v_p, z_p = _swa_last_rows(proj, cos_p, sin_p, batch=batch, blk=window, n_heads=n_heads, n_kv=n_kv,
                                           conv_dim=conv_dim)
            kp.append(k_p.reshape(batch, window, n_kv, HEAD_DIM))
            vp.append(v_p.reshape(batch, window, n_kv, HEAD_DIM))
            cp.append(z_p[:, SUBLANES - (CONV_WIDTH - 1):, :])

            proj_s = proj[n_prompt:n_prompt + dec_batch].astype(F32)
            q_rep = jnp.tile(proj_s[:, :q_w].reshape(dec_batch, n_heads, HEAD_DIM), (1, 1, n_kv))
            kv_new = proj_s[:, q_w:q_w + 2 * kv_w].reshape(dec_batch, 2, kv_w)
            gates = proj_s[:, q_w + 2 * kv_w:].reshape(dec_batch, 3, conv_dim)
            attn_s, conv_s, k_s, v_s, c_s = _swa_conv_step(
                q_rep, kv_new, gates, cache_k, cache_v, state_conv, cos_s, sin_s, sinks3, conv_w, j,
                nb=8, n_heads=n_heads, n_kv=n_kv)
            cat_s = jnp.concatenate([attn_s[:, :, :HEAD_DIM].reshape(dec_batch, q_w),
                                     conv_s.reshape(dec_batch, conv_dim)], axis=1)
            ks.append(k_s.reshape(dec_batch, window, n_kv, HEAD_DIM))
            vs.append(v_s.reshape(dec_batch, window, n_kv, HEAD_DIM))
            cs.append(c_s)
            x, h = _swa_out(proj, cos_p, sin_p, attn_sinks[j], conv_w, j, _tail_block(cat_s), w_out_b, x,
                            g_mix_post3, i, g_ffn_pre3, i, blk=window, n_heads=n_heads, n_kv=n_kv)
        else:
            z, w_out_b = _panel_matmul(h, w_in_cmlp, j, w_out_cmlp, j, act=_gelu_exact, tn=1024, name="cmlp_in")
            up.append(_cmlp_last_v(z, ln_g3, ln_b3, j, batch=batch, n_chunks=seq // chunk, chunk=chunk))
            x, h, v_tail = _cmlp_out(z, ln_g3, ln_b3, w_spatial, b_s4, step_scale, step_bias, j, w_out_b, x,
                                     g_mix_post3, i, g_ffn_pre3, i)
            us.append(v_tail[:dec_batch].reshape(dec_batch, dec_seq, cmlp_w))
        hidden, w_down_b = _panel_matmul(h, w_ffn_up, i, w_ffn_down, i, act=_relu_sq, tn=1024, name="ffn_up",
                                         panel_major=True)
        if i + 1 < depth:
            x, h = _kstream_matmul(hidden, w_down_b, x, g_ffn_post3, i, g_mix_pre3, i + 1,
                                   emit_next=True, name="ffn_down", n_tiles=FFN_DOWN_ROW_TILES, row_blocks=4,
                                   k_panels=2)
    last = depth - 1
    prompt_tile = n_prompt // KSTREAM_ROW_TILES
    (y_prompt,) = _kstream_matmul(hidden, w_down_b, x, g_ffn_post3, last, g_ffn_post3, last, emit_next=False,
                                  name="ffn_down_prompt", tm=prompt_tile, row_blocks=4, k_panels=2)
    (y_tail,) = _kstream_matmul(hidden, w_down_b, x, g_ffn_post3, last, g_ffn_post3, last, emit_next=False,
                                name="ffn_down_tail", tm=TAIL_ROWS, n_tiles=1, first_tile=n_prompt // TAIL_ROWS,
                                row_blocks=1, k_panels=2)

    return (y_prompt.reshape(batch, seq, d_model), y_tail[:dec_batch].reshape(dec_batch, dec_seq, d_model),
            jnp.stack(kp), jnp.stack(vp), jnp.stack(cp), jnp.stack(up),
            jnp.stack(ks), jnp.stack(vs), jnp.stack(cs), jnp.stack(us))
```
